```python
import jax, jax.numpy as jnp
from jax import lax
import numpy as np

D_MODEL = 2048
BATCH = 2
SEQ = 16384
DEPTH = 4
DEC_BATCH = 32
DEC_SEQ = 16
PAST_LEN = 1024

CHUNK = 64
LEFT_CHUNKS = 8
BAND_PAST = LEFT_CHUNKS * CHUNK
BAND = BAND_PAST + CHUNK
REL_CLIP = 128
TOK_W = 1024
A_HEADS = 8
A_DIM = 128
B_HEADS = 4
B_DIM = 256
N_MEM_HEADS = 4
MEM_DIM = 128
MEM_W = N_MEM_HEADS * MEM_DIM
IN_W = 3 * TOK_W + MEM_W
N_MEM = 256
D_FF = 3072
SB_BLOCK = 128
N_A_LAYERS = (DEPTH + 1) // 2
N_B_LAYERS = DEPTH // 2
EPS = 1e-6
NEG_INF = -1e30
SB_NEG = -1e4

kernel_name = 'hybrid_streaming_band_stickbreak_encoder'


def _normalize(x):
    x32 = x.astype(jnp.float32)
    r = lax.rsqrt(jnp.mean(x32 * x32, axis=-1, keepdims=True) + EPS)
    return (x32 * r).astype(x.dtype)


def _rms(x, g):
    return (_normalize(x).astype(jnp.float32) * g.astype(jnp.float32)).astype(x.dtype)


def _ffn(x, g, w_in, w_out):
    n = _normalize(x)
    w = g[:, None] * w_in
    a = n @ w[:, :D_FF]
    b = n @ w[:, D_FF:]
    return (jax.nn.silu(a) * b) @ (0.5 * w_out)


def _project(x, g, w_in, heads, dim):
    B, T, _ = x.shape
    n = _normalize(x)
    w = g[:, None] * w_in
    q = (n @ w[:, :TOK_W]).reshape(B, T, heads, dim)
    k = (n @ w[:, TOK_W:2 * TOK_W]).reshape(B, T, heads, dim)
    v = (n @ w[:, 2 * TOK_W:3 * TOK_W]).reshape(B, T, heads, dim)
    qm = (n @ w[:, 3 * TOK_W:]).reshape(B, T, N_MEM_HEADS, MEM_DIM)
    return q, k, v, qm


def _merge(tok, mo, w_out):
    B, T = tok.shape[:2]
    return jnp.concatenate([tok.reshape(B, T, TOK_W), mo.reshape(B, T, MEM_W)], axis=-1) @ w_out


def _mem_kv(mem, g, w_kv, k_gain):
    B, M, _ = mem.shape
    kv = _normalize(mem) @ (g[:, None] * w_kv)
    mk = _rms(kv[..., :MEM_W].reshape(B, M, N_MEM_HEADS, MEM_DIM), k_gain)
    mv = kv[..., MEM_W:].reshape(B, M, N_MEM_HEADS, MEM_DIM)
    return mk, mv


def _mem_attn(qm, q_gain, mk, mv):
    q = _rms(qm, q_gain) * (MEM_DIM ** -0.5)
    s = jnp.einsum('bqhd,bmhd->bhqm', q, mk).astype(jnp.float32)
    p = jax.nn.softmax(s, axis=-1).astype(mv.dtype)
    return jnp.einsum('bhqm,bmhd->bqhd', p, mv)


def _band_bias(rel_bias, q_pos, k_pos):
    rel = jnp.clip(q_pos[:, None] - k_pos[None, :], -REL_CLIP, REL_CLIP) + REL_CLIP
    qc = q_pos // CHUNK
    kc = k_pos // CHUNK
    valid = (k_pos[None, :] >= 0) & (kc[None, :] <= qc[:, None]) & (kc[None, :] >= qc[:, None] - LEFT_CHUNKS)
    return jnp.where(valid[None], rel_bias[:, rel].astype(jnp.float32), NEG_INF)


def _band_core(q, k, v, bias):
    s = jnp.einsum('bqhd,bkhd->bhqk', q, k).astype(jnp.float32) + bias[None]
    p = jax.nn.softmax(s, axis=-1).astype(v.dtype)
    return jnp.einsum('bhqk,bkhd->bqhd', p, v)


def _band_prompt(q, k, v, rel_bias):
    B, T, H, Dh = q.shape
    pad = ((0, 0), (BAND_PAST, 0), (0, 0), (0, 0))
    k_pad = jnp.pad(k, pad)
    v_pad = jnp.pad(v, pad)
    bias0 = _band_bias(rel_bias, BAND_PAST + jnp.arange(CHUNK), jnp.arange(BAND))

    def one_chunk(c):
        start = c * CHUNK
        q_c = lax.dynamic_slice_in_dim(q, start, CHUNK, axis=1)
        k_c = lax.dynamic_slice_in_dim(k_pad, start, BAND, axis=1)
        v_c = lax.dynamic_slice_in_dim(v_pad, start, BAND, axis=1)
        k_pos = start - BAND_PAST + jnp.arange(BAND)
        bias = bias0 + jnp.where(k_pos >= 0, 0.0, NEG_INF)[None, None, :]
        return _band_core(q_c, k_c, v_c, bias)

    out = lax.map(one_chunk, jnp.arange(T // CHUNK))
    return jnp.moveaxis(out, 0, 1).reshape(B, T, H, Dh)


def _sb_attend(q, kp, vp, kd, vd, diag_mask):
    f32 = jnp.float32
    zd = jnp.where(diag_mask[None, None], jnp.einsum('bqhd,bkhd->bhqk', q, kd).astype(f32), SB_NEG)
    td = zd.shape[-1]
    u_d = (jnp.arange(td)[:, None] >= jnp.arange(td)[None, :]).astype(f32)
    loc_d = jnp.log1p(jnp.exp(zd)) @ u_d
    out = jnp.einsum('bhqk,bkhd->bqhd', jnp.exp(zd - loc_d).astype(vd.dtype), vd)
    n = kp.shape[1]
    if n:
        zo = jnp.einsum('bqhd,bnkhd->bhqnk', q, kp).astype(f32)
        u = (jnp.arange(SB_BLOCK)[:, None] >= jnp.arange(SB_BLOCK)[None, :]).astype(f32)
        loc_o = jnp.einsum('bhqnk,kj->bhqnj', jnp.log1p(jnp.exp(zo)), u)
        later = (jnp.arange(n)[:, None] > jnp.arange(n)[None, :]).astype(f32)
        aft = jnp.einsum('bhqm,mn->bhqn', loc_o[..., 0], later) + loc_d[..., 0:1]
        w = jnp.exp(zo - loc_o - aft[..., None]).astype(vp.dtype)
        out = out + jnp.einsum('bhqnk,bnkhd->bqhd', w, vp)
    return out


def _sb_prompt(q, k, v):
    B, T, H, Dh = q.shape
    mask = jnp.arange(SB_BLOCK)[None, :] < jnp.arange(SB_BLOCK)[:, None]
    outs = []
    for i in range(T // SB_BLOCK):
        lo, hi = i * SB_BLOCK, (i + 1) * SB_BLOCK
        kp = k[:, :lo].reshape(B, i, SB_BLOCK, H, Dh)
        vp = v[:, :lo].reshape(B, i, SB_BLOCK, H, Dh)
        outs.append(_sb_attend(q[:, lo:hi], kp, vp, k[:, lo:hi], v[:, lo:hi], mask))
    return jnp.concatenate(outs, axis=1)


def setup_inputs(seed: int = 0) -> dict:
    key = jax.random.key(seed)
    ks = jax.random.split(key, 32)
    f32 = jnp.float32
    a_cache = min(BAND_PAST, PAST_LEN)
    nrm = lambda k, shape, scale=1.0: jax.random.normal(k, shape, f32) * scale
    gain = lambda k, shape: 1.0 + 0.02 * jax.random.normal(k, shape, f32)
    return {
        'x_prompt': nrm(ks[0], (BATCH, SEQ, D_MODEL)),
        'x_sample': nrm(ks[1], (DEC_BATCH, DEC_SEQ, D_MODEL)),
        'mem_prompt': nrm(ks[2], (BATCH, N_MEM, D_MODEL)),
        'cache_a_k': nrm(ks[3], (N_A_LAYERS, DEC_BATCH, a_cache, A_HEADS, A_DIM)),
        'cache_a_v': nrm(ks[4], (N_A_LAYERS, DEC_BATCH, a_cache, A_HEADS, A_DIM)),
        'cache_b_k': nrm(ks[5], (N_B_LAYERS, DEC_BATCH, PAST_LEN, B_HEADS, B_DIM)),
        'cache_b_v': nrm(ks[6], (N_B_LAYERS, DEC_BATCH, PAST_LEN, B_HEADS, B_DIM)),
        'cache_mem_k': nrm(ks[7], (DEPTH, DEC_BATCH, N_MEM, N_MEM_HEADS, MEM_DIM)),
        'cache_mem_v': nrm(ks[8], (DEPTH, DEC_BATCH, N_MEM, N_MEM_HEADS, MEM_DIM)),
        'ffn1_norm': gain(ks[9], (DEPTH, D_MODEL)),
        'ffn1_w_in': nrm(ks[10], (DEPTH, D_MODEL, 2 * D_FF), D_MODEL ** -0.5),
        'ffn1_w_out': nrm(ks[11], (DEPTH, D_FF, D_MODEL), D_FF ** -0.5),
        'attn_norm': gain(ks[12], (DEPTH, D_MODEL)),
        'w_in': nrm(ks[13], (DEPTH, D_MODEL, IN_W), D_MODEL ** -0.5),
        'w_out': nrm(ks[14], (DEPTH, TOK_W + MEM_W, D_MODEL), (TOK_W + MEM_W) ** -0.5),
        'a_q_gain': gain(ks[15], (N_A_LAYERS, A_DIM)),
        'a_k_gain': gain(ks[16], (N_A_LAYERS, A_DIM)),
        'a_rel_bias': nrm(ks[17], (N_A_LAYERS, A_HEADS, 2 * REL_CLIP + 1), 0.1),
        'mem_norm': gain(ks[18], (DEPTH, D_MODEL)),
        'w_mem_kv': nrm(ks[19], (DEPTH, D_MODEL, 2 * MEM_W), D_MODEL ** -0.5),
        'mem_q_gain': gain(ks[20], (DEPTH, MEM_DIM)),
        'mem_k_gain': gain(ks[21], (DEPTH, MEM_DIM)),
        'ffn2_norm': gain(ks[22], (DEPTH, D_MODEL)),
        'ffn2_w_in': nrm(ks[23], (DEPTH, D_MODEL, 2 * D_FF), D_MODEL ** -0.5),
        'ffn2_w_out': nrm(ks[24], (DEPTH, D_FF, D_MODEL), D_FF ** -0.5),
    }


def reference(x_prompt, x_sample, mem_prompt, cache_a_k, cache_a_v, cache_b_k, cache_b_v,
              cache_mem_k, cache_mem_v, ffn1_norm, ffn1_w_in, ffn1_w_out, attn_norm, w_in, w_out,
              a_q_gain, a_k_gain, a_rel_bias, mem_norm, w_mem_kv, mem_q_gain, mem_k_gain,
              ffn2_norm, ffn2_w_in, ffn2_w_out):
    past = cache_b_k.shape[2]
    a_cache = cache_a_k.shape[2]
    n_dec, t_s = x_sample.shape[0], x_sample.shape[1]
    a_keep = min(BAND_PAST, x_prompt.shape[1])
    q_pos_s = past + jnp.arange(t_s)
    k_pos_a_s = jnp.concatenate([jnp.arange(past - a_cache, past), q_pos_s])
    sb_mask_s = jnp.arange(t_s)[None, :] < jnp.arange(t_s)[:, None]

    x_p, x_s = x_prompt, x_sample
    a_k_p, a_v_p, b_k_p, b_v_p, m_k_p, m_v_p = [], [], [], [], [], []
    a_k_s, a_v_s, b_k_s, b_v_s = [], [], [], []
    for l in range(DEPTH):
        j = l // 2
        x_p = x_p + _ffn(x_p, ffn1_norm[l], ffn1_w_in[l], ffn1_w_out[l])
        x_s = x_s + _ffn(x_s, ffn1_norm[l], ffn1_w_in[l], ffn1_w_out[l])
        if l % 2 == 0:
            qp, kp, vp, qmp = _project(x_p, attn_norm[l], w_in[l], A_HEADS, A_DIM)
            qs, ks_, vs, qms = _project(x_s, attn_norm[l], w_in[l], A_HEADS, A_DIM)
            qp, kp = _rms(qp, a_q_gain[j]) * (A_DIM ** -0.5), _rms(kp, a_k_gain[j])
            qs, ks_ = _rms(qs, a_q_gain[j]) * (A_DIM ** -0.5), _rms(ks_, a_k_gain[j])
            tok_p = _band_prompt(qp, kp, vp, a_rel_bias[j])
            k_all = jnp.concatenate([cache_a_k[j], ks_], axis=1)
            v_all = jnp.concatenate([cache_a_v[j], vs], axis=1)
            tok_s = _band_core(qs, k_all, v_all, _band_bias(a_rel_bias[j], q_pos_s, k_pos_a_s))
            a_k_p.append(kp[:, kp.shape[1] - a_keep:])
            a_v_p.append(vp[:, vp.shape[1] - a_keep:])
            a_k_s.append(ks_)
            a_v_s.append(vs)
        else:
            qp, kp, vp, qmp = _project(x_p, attn_norm[l], w_in[l], B_HEADS, B_DIM)
            qs, ks_, vs, qms = _project(x_s, attn_norm[l], w_in[l], B_HEADS, B_DIM)
            tok_p = _sb_prompt(qp * (B_DIM ** -0.5), kp, vp)
            kc = cache_b_k[j].reshape(n_dec, past // SB_BLOCK, SB_BLOCK, B_HEADS, B_DIM)
            vc = cache_b_v[j].reshape(n_dec, past // SB_BLOCK, SB_BLOCK, B_HEADS, B_DIM)
            tok_s = _sb_attend(qs * (B_DIM ** -0.5), kc, vc, ks_, vs, sb_mask_s)
            b_k_p.append(kp)
            b_v_p.append(vp)
            b_k_s.append(ks_)
            b_v_s.append(vs)
        mk_p, mv_p = _mem_kv(mem_prompt, mem_norm[l], w_mem_kv[l], mem_k_gain[l])
        m_k_p.append(mk_p)
        m_v_p.append(mv_p)
        mo_p = _mem_attn(qmp, mem_q_gain[l], mk_p, mv_p)
        mo_s = _mem_attn(qms, mem_q_gain[l], cache_mem_k[l], cache_mem_v[l])
        x_p = x_p + _merge(tok_p, mo_p, w_out[l])
        x_s = x_s + _merge(tok_s, mo_s, w_out[l])
        x_p = x_p + _ffn(x_p, ffn2_norm[l], ffn2_w_in[l], ffn2_w_out[l])
        x_s = x_s + _ffn(x_s, ffn2_norm[l], ffn2_w_in[l], ffn2_w_out[l])

    return (x_p, x_s,
            jnp.stack(a_k_p), jnp.stack(a_v_p), jnp.stack(b_k_p), jnp.stack(b_v_p),
            jnp.stack(m_k_p), jnp.stack(m_v_p),
            jnp.stack(a_k_s), jnp.stack(a_v_s), jnp.stack(b_k_s), jnp.stack(b_v_s))
```

```python
import functools

import jax
import jax.numpy as jnp
from jax import lax
from jax.experimental import pallas as pl
from jax.experimental.pallas import tpu as pltpu

CHUNK = 64
LEFT_CHUNKS = 8
BAND_PAST = LEFT_CHUNKS * CHUNK
REL_CLIP = 128
TOK_W = 1024
A_HEADS, A_DIM = 8, 128
B_HEADS, B_DIM = 4, 256
N_MEM_HEADS, MEM_DIM = 4, 128
MEM_W = N_MEM_HEADS * MEM_DIM
EPS = 1e-6
NEG_INF = -1e30
SB_NEG = -1e4

V7X_VMEM_LIMIT_BYTES = 56 * 1024 * 1024
TOKEN_TILE = 512
FF_TILE = 512
BAND_Q_TILE = BAND_PAST // 2
SB_Q_TILE = 512
SB_K_TILE = 256

BF16 = jnp.bfloat16
F32 = jnp.float32


def _params(*semantics):
    return pltpu.CompilerParams(dimension_semantics=semantics, vmem_limit_bytes=V7X_VMEM_LIMIT_BYTES)


def _dot(a, b):
    return jnp.dot(a, b, preferred_element_type=F32)


def _dot_t(a, b):
    return lax.dot_general(a, b, (((1,), (1,)), ((), ())), preferred_element_type=F32)


def _normed_rows(x, g):
    r = lax.rsqrt(jnp.mean(x * x, axis=-1, keepdims=True) + EPS)
    return (x * r * g).astype(BF16)


def _softplus(z):
    return jnp.log1p(jnp.exp(z))


def _ffn_kernel(x_ref, g_ref, wa_ref, wb_ref, wo_ref, o_ref, n_ref, acc_ref):
    j = pl.program_id(1)

    @pl.when(j == 0)
    def _():
        n_ref[...] = _normed_rows(x_ref[...], g_ref[...])
        acc_ref[...] = jnp.zeros_like(acc_ref)

    n = n_ref[...]
    a = _dot(n, wa_ref[...])
    b = _dot(n, wb_ref[...])
    h = (a / (1.0 + jnp.exp(-a)) * b).astype(BF16)
    acc_ref[...] += _dot(h, wo_ref[...])

    @pl.when(j == pl.num_programs(1) - 1)
    def _():
        o_ref[...] = x_ref[...] + 0.5 * acc_ref[...]


def _ffn(x, g, w_in, w_out):
    n, d = x.shape
    f = w_out.shape[0]
    tm = min(TOKEN_TILE, n)
    tf = min(FF_TILE, f)
    nf = f // tf
    return pl.pallas_call(
        _ffn_kernel,
        grid=(n // tm, nf),
        in_specs=[
            pl.BlockSpec((tm, d), lambda i, j: (i, 0)),
            pl.BlockSpec((1, d), lambda i, j: (0, 0)),
            pl.BlockSpec((d, tf), lambda i, j: (0, j)),
            pl.BlockSpec((d, tf), lambda i, j: (0, j + nf)),
            pl.BlockSpec((tf, d), lambda i, j: (j, 0)),
        ],
        out_specs=pl.BlockSpec((tm, d), lambda i, j: (i, 0)),
        out_shape=jax.ShapeDtypeStruct((n, d), F32),
        scratch_shapes=[pltpu.VMEM((tm, d), BF16), pltpu.VMEM((tm, d), F32)],
        compiler_params=_params("parallel", "arbitrary"),
        name="ffn",
    )(x, g.reshape(1, d), w_in, w_in, w_out)


def _proj_kernel(*refs, segments, group):
    n_gain = 1 + max([s[3] for s in segments if s[3] is not None], default=-1)
    x_ref, g_ref, w_ref = refs[:3]
    gain_refs = refs[3:3 + n_gain]
    out_refs = refs[3 + n_gain:]
    n = _normed_rows(x_ref[...], g_ref[...])
    o = 0
    for col0, width, head_dim, gain_idx, scale, n_outs in segments:
        for c in range(0, width, group):
            y = _dot(n, w_ref[:, col0 + c:col0 + c + group])
            for h0 in range(0, group, head_dim):
                yh = y[:, h0:h0 + head_dim]
                if gain_idx is not None:
                    r = lax.rsqrt(jnp.mean(yh * yh, axis=-1, keepdims=True) + EPS)
                    yh = yh * r * gain_refs[gain_idx][...]
                if scale != 1.0:
                    yh = yh * scale
                for t in range(n_outs):
                    ref = out_refs[o + t]
                    ref[:, c + h0:c + h0 + head_dim] = yh.astype(ref.dtype)
        o += n_outs


def _proj(x, g, w, segments, gains, out_dtypes):
    n, d = x.shape
    tm = min(TOKEN_TILE, n)
    group = 512
    segs = tuple((c0, wd, hd, gi, sc, len(out_dtypes[s])) for s, (c0, wd, hd, gi, sc) in enumerate(segments))
    out_shape, out_specs = [], []
    for (c0, wd, hd, gi, sc), dts in zip(segments, out_dtypes):
        for dt in dts:
            out_shape.append(jax.ShapeDtypeStruct((n, wd), dt))
            out_specs.append(pl.BlockSpec((tm, wd), lambda i: (i, 0)))
    in_specs = [
        pl.BlockSpec((tm, d), lambda i: (i, 0)),
        pl.BlockSpec((1, d), lambda i: (0, 0)),
        pl.BlockSpec(w.shape, lambda i: (0, 0), pipeline_mode=pl.Buffered(1)),
    ] + [pl.BlockSpec((1, gn.shape[-1]), lambda i: (0, 0)) for gn in gains]
    return pl.pallas_call(
        functools.partial(_proj_kernel, segments=segs, group=group),
        grid=(n // tm,),
        in_specs=in_specs,
        out_specs=out_specs,
        out_shape=out_shape,
        compiler_params=_params("parallel"),
        name="proj",
    )(x, g.reshape(1, d), w, *[gn.reshape(1, -1) for gn in gains])


def _band_kernel(q_ref, k0_ref, k1_ref, k2_ref, v0_ref, v1_ref, v2_ref, bias_ref, o_ref):
    tq = q_ref.shape[0]
    i = pl.program_id(1)
    col = lax.broadcasted_iota(jnp.int32, (tq, 3 * tq), 1)
    in_seq = col >= (2 - i) * tq
    for h in range(A_HEADS):
        sl = slice(h * A_DIM, (h + 1) * A_DIM)
        kh = jnp.concatenate([k0_ref[:, sl], k1_ref[:, sl], k2_ref[:, sl]], axis=0)
        vh = jnp.concatenate([v0_ref[:, sl], v1_ref[:, sl], v2_ref[:, sl]], axis=0)
        s = _dot_t(q_ref[:, sl], kh) + bias_ref[h]
        s = jnp.where(in_seq, s, NEG_INF)
        p = jnp.exp(s - jnp.max(s, axis=-1, keepdims=True))
        l = jnp.sum(p, axis=-1, keepdims=True)
        o_ref[:, sl] = (_dot(p.astype(BF16), vh) / l).astype(o_ref.dtype)


def _band_table(rel_bias, q_pos, k_pos):
    rel = jnp.clip(q_pos[:, None] - k_pos[None, :], -REL_CLIP, REL_CLIP) + REL_CLIP
    qc = q_pos // CHUNK
    kc = k_pos // CHUNK
    valid = (k_pos[None, :] >= 0) & (kc[None, :] <= qc[:, None]) & (kc[None, :] >= qc[:, None] - LEFT_CHUNKS)
    return jnp.where(valid[None], rel_bias[:, rel].astype(F32), NEG_INF)


def _band_prompt(q, k, v, rel_bias):
    b, t, w = q.shape
    tq = BAND_Q_TILE
    bias = _band_table(rel_bias, 2 * tq + jnp.arange(tq), jnp.arange(3 * tq))
    qspec = pl.BlockSpec((None, tq, w), lambda bi, i: (bi, i, 0))
    kspecs = [pl.BlockSpec((None, tq, w), functools.partial(lambda bi, i, d: (bi, jnp.maximum(i - d, 0), 0), d=d))
              for d in (2, 1, 0)]
    return pl.pallas_call(
        _band_kernel,
        grid=(b, t // tq),
        in_specs=[qspec] + kspecs + kspecs + [
            pl.BlockSpec(bias.shape, lambda bi, i: (0, 0, 0), pipeline_mode=pl.Buffered(1))],
        out_specs=qspec,
        out_shape=jax.ShapeDtypeStruct((b, t, w), BF16),
        compiler_params=_params("parallel", "parallel"),
        name="band_prompt",
    )(q, k, k, k, v, v, v, bias)


def _band_step_kernel(q_ref, ck_ref, cv_ref, nk_ref, nv_ref, bc_ref, bn_ref, o_ref):
    for h in range(A_HEADS):
        sl = slice(h * A_DIM, (h + 1) * A_DIM)
        qh = q_ref[:, sl]
        sc = _dot_t(qh, ck_ref[:, sl].astype(BF16)) + bc_ref[h]
        sn = _dot_t(qh, nk_ref[:, sl]) + bn_ref[h]
        m = jnp.maximum(jnp.max(sc, axis=-1, keepdims=True), jnp.max(sn, axis=-1, keepdims=True))
        pc = jnp.exp(sc - m)
        pn = jnp.exp(sn - m)
        l = jnp.sum(pc, axis=-1, keepdims=True) + jnp.sum(pn, axis=-1, keepdims=True)
        o = _dot(pc.astype(BF16), cv_ref[:, sl].astype(BF16)) + _dot(pn.astype(BF16), nv_ref[:, sl])
        o_ref[:, sl] = (o / l).astype(o_ref.dtype)


def _band_step(q, ck, cv, nk, nv, rel_bias, past):
    n, ts, w = q.shape
    c = ck.shape[1]
    q_pos = past + jnp.arange(ts)
    bias_c = _band_table(rel_bias, q_pos, jnp.arange(past - c, past))
    bias_n = _band_table(rel_bias, q_pos, q_pos)
    new = pl.BlockSpec((None, ts, w), lambda i: (i, 0, 0))
    old = pl.BlockSpec((None, c, w), lambda i: (i, 0, 0))
    return pl.pallas_call(
        _band_step_kernel,
        grid=(n,),
        in_specs=[new, old, old, new, new,
                  pl.BlockSpec(bias_c.shape, lambda i: (0, 0, 0)),
                  pl.BlockSpec(bias_n.shape, lambda i: (0, 0, 0))],
        out_specs=new,
        out_shape=jax.ShapeDtypeStruct((n, ts, w), BF16),
        compiler_params=_params("parallel"),
        name="band_step",
    )(q, ck, cv, nk, nv, bias_c, bias_n)


def _suffix_ones(n):
    return (jnp.arange(n)[:, None] >= jnp.arange(n)[None, :]).astype(BF16)


def _sb_block(q, kb, vb, u, aft, mask):
    z = _dot_t(q, kb)
    if mask is not None:
        z = jnp.where(mask, z, SB_NEG)
    loc = _dot(_softplus(z).astype(BF16), u)
    w = jnp.exp(z - loc - aft)
    return _dot(w.astype(BF16), vb), aft + loc[:, 0:1]


def _sb_kernel(q_ref, k_ref, v_ref, u_ref, o_ref, acc_ref, aft_ref):
    tq = q_ref.shape[0]
    tk = u_ref.shape[0]
    i = pl.program_id(2)
    q = q_ref[...]
    acc_ref[...] = jnp.zeros_like(acc_ref)
    aft_ref[...] = jnp.zeros_like(aft_ref)
    n_blocks = (i + 1) * (tq // tk)
    q_pos = i * tq + lax.broadcasted_iota(jnp.int32, (tq, tk), 0)
    k_off = lax.broadcasted_iota(jnp.int32, (tq, tk), 1)

    def body(step, carry):
        k0 = pl.multiple_of((n_blocks - 1 - step) * tk, tk)
        out, aft = _sb_block(q, k_ref[pl.ds(k0, tk), :], v_ref[pl.ds(k0, tk), :], u_ref[...],
                             aft_ref[...], k0 + k_off < q_pos)
        acc_ref[...] += out
        aft_ref[...] = aft
        return carry

    lax.fori_loop(0, n_blocks, body, 0)
    o_ref[...] = acc_ref[...].astype(o_ref.dtype)


def _sb_prompt(q, k, v):
    b, t, w = q.shape
    tq = min(SB_Q_TILE, t)
    tk = min(SB_K_TILE, tq)
    qspec = pl.BlockSpec((None, tq, B_DIM), lambda bi, h, i: (bi, i, h))
    kspec = pl.BlockSpec((None, t, B_DIM), lambda bi, h, i: (bi, 0, h))
    return pl.pallas_call(
        _sb_kernel,
        grid=(b, B_HEADS, t // tq),
        in_specs=[qspec, kspec, kspec, pl.BlockSpec((tk, tk), lambda bi, h, i: (0, 0))],
        out_specs=qspec,
        out_shape=jax.ShapeDtypeStruct((b, t, w), BF16),
        scratch_shapes=[pltpu.VMEM((tq, B_DIM), F32), pltpu.VMEM((tq, 1), F32)],
        compiler_params=_params("parallel", "parallel", "arbitrary"),
        name="sb_prompt",
    )(q, k, v, _suffix_ones(tk))


def _sb_step_kernel(q_ref, ck_ref, cv_ref, nk_ref, nv_ref, un_ref, uc_ref, o_ref):
    ts = q_ref.shape[0]
    tk = uc_ref.shape[0]
    n_cached = ck_ref.shape[0] // tk
    causal = lax.broadcasted_iota(jnp.int32, (ts, ts), 1) < lax.broadcasted_iota(jnp.int32, (ts, ts), 0)
    for h in range(B_HEADS):
        sl = slice(h * B_DIM, (h + 1) * B_DIM)
        q = q_ref[:, sl]
        acc, aft = _sb_block(q, nk_ref[:, sl], nv_ref[:, sl], un_ref[...], jnp.zeros((ts, 1), F32), causal)
        for j in reversed(range(n_cached)):
            rows = slice(j * tk, (j + 1) * tk)
            out, aft = _sb_block(q, ck_ref[rows, sl].astype(BF16), cv_ref[rows, sl].astype(BF16),
                                 uc_ref[...], aft, None)
            acc = acc + out
        o_ref[:, sl] = acc.astype(o_ref.dtype)


def _sb_step(q, ck, cv, nk, nv):
    n, ts, w = q.shape
    p = ck.shape[1]
    tk = min(SB_K_TILE, p)
    new = pl.BlockSpec((None, ts, w), lambda i: (i, 0, 0))
    old = pl.BlockSpec((None, p, w), lambda i: (i, 0, 0))
    return pl.pallas_call(
        _sb_step_kernel,
        grid=(n,),
        in_specs=[new, old, old, new, new,
                  pl.BlockSpec((ts, ts), lambda i: (0, 0)),
                  pl.BlockSpec((tk, tk), lambda i: (0, 0))],
        out_specs=new,
        out_shape=jax.ShapeDtypeStruct((n, ts, w), BF16),
        compiler_params=_params("parallel"),
        name="sb_step",
    )(q, ck, cv, nk, nv, _suffix_ones(ts), _suffix_ones(tk))


def _merge_kernel(x_ref, tok_ref, qm_ref, mk_ref, mv_ref, w_ref, o_ref, mo_ref):
    for h in range(N_MEM_HEADS):
        sl = slice(h * MEM_DIM, (h + 1) * MEM_DIM)
        s = _dot_t(qm_ref[:, sl], mk_ref[:, sl].astype(BF16))
        p = jnp.exp(s - jnp.max(s, axis=-1, keepdims=True))
        l = jnp.sum(p, axis=-1, keepdims=True)
        mo_ref[:, sl] = (_dot(p.astype(BF16), mv_ref[:, sl].astype(BF16)) / l).astype(BF16)
    y = _dot(tok_ref[...], w_ref[:TOK_W, :]) + _dot(mo_ref[...], w_ref[TOK_W:, :])
    o_ref[...] = x_ref[...] + y


def _merge(x, tok, qm, mk, mv, w_out):
    b, t, d = x.shape
    m = mk.shape[1]
    tm = min(TOKEN_TILE, t)
    row = lambda width: pl.BlockSpec((None, tm, width), lambda bi, i: (bi, i, 0))
    mem = pl.BlockSpec((None, m, MEM_W), lambda bi, i: (bi, 0, 0))
    return pl.pallas_call(
        _merge_kernel,
        grid=(b, t // tm),
        in_specs=[row(d), row(TOK_W), row(MEM_W), mem, mem,
                  pl.BlockSpec(w_out.shape, lambda bi, i: (0, 0), pipeline_mode=pl.Buffered(1))],
        out_specs=row(d),
        out_shape=jax.ShapeDtypeStruct((b, t, d), F32),
        scratch_shapes=[pltpu.VMEM((tm, MEM_W), BF16)],
        compiler_params=_params("parallel", "parallel"),
        name="merge",
    )(x, tok, qm, mk, mv, w_out)


def kernel(x_prompt, x_sample, mem_prompt, cache_a_k, cache_a_v, cache_b_k, cache_b_v, cache_mem_k, cache_mem_v, ffn1_norm, ffn1_w_in, ffn1_w_out, attn_norm, w_in, w_out, a_q_gain, a_k_gain, a_rel_bias, mem_norm, w_mem_kv, mem_q_gain, mem_k_gain, ffn2_norm, ffn2_w_in, ffn2_w_out):
    depth = w_in.shape[0]
    bp, tp, d = x_prompt.shape
    bs, ts, _ = x_sample.shape
    n_mem = mem_prompt.shape[1]
    past = cache_b_k.shape[2]
    a_keep = min(BAND_PAST, tp)

    ffn1_w_in, ffn1_w_out, ffn2_w_in, ffn2_w_out, w_in, w_out, w_mem_kv = (
        w.astype(BF16) for w in (ffn1_w_in, ffn1_w_out, ffn2_w_in, ffn2_w_out, w_in, w_out, w_mem_kv))

    x_p = x_prompt.reshape(bp * tp, d)
    x_s = x_sample.reshape(bs * ts, d)
    mem = mem_prompt.reshape(bp * n_mem, d)
    a_k_p, a_v_p, b_k_p, b_v_p, m_k_p, m_v_p = [], [], [], [], [], []
    a_k_s, a_v_s, b_k_s, b_v_s = [], [], [], []
    for l in range(depth):
        j = l // 2
        band = l % 2 == 0
        heads, dim = (A_HEADS, A_DIM) if band else (B_HEADS, B_DIM)
        x_p = _ffn(x_p, ffn1_norm[l], ffn1_w_in[l], ffn1_w_out[l])
        x_s = _ffn(x_s, ffn1_norm[l], ffn1_w_in[l], ffn1_w_out[l])

        segments = [(0, TOK_W, dim, 0 if band else None, dim ** -0.5),
                    (TOK_W, TOK_W, dim, 1 if band else None, 1.0),
                    (2 * TOK_W, TOK_W, dim, None, 1.0),
                    (3 * TOK_W, MEM_W, MEM_DIM, 2 if band else 0, MEM_DIM ** -0.5)]
        gains = [a_q_gain[j], a_k_gain[j], mem_q_gain[l]] if band else [mem_q_gain[l]]
        dtypes = [[BF16], [BF16, F32], [BF16, F32], [BF16]]
        q_p, k_p, kf_p, v_p, vf_p, qm_p = _proj(x_p, attn_norm[l], w_in[l], segments, gains, dtypes)
        q_s, k_s, kf_s, v_s, vf_s, qm_s = _proj(x_s, attn_norm[l], w_in[l], segments, gains, dtypes)
        in_p = lambda a: a.reshape(bp, tp, -1)
        in_s = lambda a: a.reshape(bs, ts, -1)

        if band:
            tok_p = _band_prompt(in_p(q_p), in_p(k_p), in_p(v_p), a_rel_bias[j])
            tok_s = _band_step(in_s(q_s), cache_a_k[j].reshape(bs, -1, TOK_W), cache_a_v[j].reshape(bs, -1, TOK_W),
                               in_s(k_s), in_s(v_s), a_rel_bias[j], past)
            a_k_p.append(kf_p.reshape(bp, tp, heads, dim)[:, tp - a_keep:])
            a_v_p.append(vf_p.reshape(bp, tp, heads, dim)[:, tp - a_keep:])
            a_k_s.append(kf_s.reshape(bs, ts, heads, dim))
            a_v_s.append(vf_s.reshape(bs, ts, heads, dim))
        else:
            tok_p = _sb_prompt(in_p(q_p), in_p(k_p), in_p(v_p))
            tok_s = _sb_step(in_s(q_s), cache_b_k[j].reshape(bs, past, TOK_W), cache_b_v[j].reshape(bs, past, TOK_W),
                             in_s(k_s), in_s(v_s))
            b_k_p.append(kf_p.reshape(bp, tp, heads, dim))
            b_v_p.append(vf_p.reshape(bp, tp, heads, dim))
            b_k_s.append(kf_s.reshape(bs, ts, heads, dim))
            b_v_s.append(vf_s.reshape(bs, ts, heads, dim))

        mem_segments = [(0, MEM_W, MEM_DIM, 0, 1.0), (MEM_W, MEM_W, MEM_DIM, None, 1.0)]
        mk, mkf, mv, mvf = _proj(mem, mem_norm[l], w_mem_kv[l], mem_segments, [mem_k_gain[l]],
                                 [[BF16, F32], [BF16, F32]])
        m_k_p.append(mkf.reshape(bp, n_mem, N_MEM_HEADS, MEM_DIM))
        m_v_p.append(mvf.reshape(bp, n_mem, N_MEM_HEADS, MEM_DIM))

        x_p = _merge(in_p(x_p), tok_p, in_p(qm_p), mk.reshape(bp, n_mem, MEM_W), mv.reshape(bp, n_mem, MEM_W),
                     w_out[l]).reshape(bp * tp, d)
        x_s = _merge(in_s(x_s), tok_s, in_s(qm_s), cache_mem_k[l].reshape(bs, -1, MEM_W),
                     cache_mem_v[l].reshape(bs, -1, MEM_W), w_out[l]).reshape(bs * ts, d)

        x_p = _ffn(x_p, ffn2_norm[l], ffn2_w_in[l], ffn2_w_out[l])
        x_s = _ffn(x_s, ffn2_norm[l], ffn2_w_in[l], ffn2_w_out[l])

    return (x_p.reshape(bp, tp, d), x_s.reshape(bs, ts, d),
            jnp.stack(a_k_p), jnp.stack(a_v_p), jnp.stack(b_k_p), jnp.stack(b_v_p),
            jnp.stack(m_k_p), jnp.stack(m_v_p),
            jnp.stack(a_k_s), jnp.stack(a_v_s), jnp.stack(b_k_s), jnp.stack(b_v_s))
```

```python
import functools

import jax
import jax.numpy as jnp
import numpy as np
from jax import lax
from jax.experimental import pallas as pl
from jax.experimental.pallas import tpu as pltpu

CHUNK = 64
LEFT_CHUNKS = 8
BAND_PAST = LEFT_CHUNKS * CHUNK
REL_CLIP = 128
TOK_W = 1024
A_HEADS, A_DIM = 8, 128
B_HEADS, B_DIM = 4, 256
N_MEM_HEADS, MEM_DIM = 4, 128
MEM_W = N_MEM_HEADS * MEM_DIM
EPS = 1e-6
NEG_INF = -1e30
SB_NEG = -1e4

V7X_VMEM_LIMIT_BYTES = 56 * 1024 * 1024
TOKEN_TILE = 512
FF_TILE = 1024
BAND_Q_TILE = BAND_PAST // 2
SB_TILE = 256
SB_EXIT_MASS = 120.0

BF16 = jnp.bfloat16
F32 = jnp.float32


def _params(*semantics):
    return pltpu.CompilerParams(dimension_semantics=semantics, vmem_limit_bytes=V7X_VMEM_LIMIT_BYTES)


def _dot(a, b):
    return jnp.dot(a, b, preferred_element_type=F32)


def _dot_t(a, b):
    return lax.dot_general(a, b, (((1,), (1,)), ((), ())), preferred_element_type=F32)


def _normed_rows(x, g):
    r = lax.rsqrt(jnp.mean(x * x, axis=-1, keepdims=True) + EPS)
    return (x * r * g).astype(BF16)


def _softplus(z):
    return jnp.log1p(jnp.exp(z))


def _ffn_kernel(x_ref, g_ref, wa_ref, wb_ref, wo_ref, o_ref, n_ref, acc_ref):
    j = pl.program_id(1)

    @pl.when(j == 0)
    def _():
        n_ref[...] = _normed_rows(x_ref[...], g_ref[...])
        acc_ref[...] = jnp.zeros_like(acc_ref)

    n = n_ref[...]
    a = _dot(n, wa_ref[...])
    b = _dot(n, wb_ref[...])
    h = (a / (1.0 + jnp.exp(-a)) * b).astype(BF16)
    acc_ref[...] += _dot(h, wo_ref[...])

    @pl.when(j == pl.num_programs(1) - 1)
    def _():
        o_ref[...] = x_ref[...] + 0.5 * acc_ref[...]


def _ffn(x, g, w_in, w_out):
    n, d = x.shape
    f = w_out.shape[0]
    tm = min(TOKEN_TILE, n)
    tf = min(FF_TILE, f)
    nf = f // tf
    return pl.pallas_call(
        _ffn_kernel,
        grid=(n // tm, nf),
        in_specs=[
            pl.BlockSpec((tm, d), lambda i, j: (i, 0)),
            pl.BlockSpec((1, d), lambda i, j: (0, 0)),
            pl.BlockSpec((d, tf), lambda i, j: (0, j)),
            pl.BlockSpec((d, tf), lambda i, j: (0, j + nf)),
            pl.BlockSpec((tf, d), lambda i, j: (j, 0)),
        ],
        out_specs=pl.BlockSpec((tm, d), lambda i, j: (i, 0)),
        out_shape=jax.ShapeDtypeStruct((n, d), F32),
        scratch_shapes=[pltpu.VMEM((tm, d), BF16), pltpu.VMEM((tm, d), F32)],
        compiler_params=_params("parallel", "arbitrary"),
        name="ffn",
    )(x, g.reshape(1, d), w_in, w_in, w_out)


def _proj_kernel(*refs, segments, group, n_gain, n_carried):
    x_ref, g_ref, w_ref = refs[:3]
    gain_refs = refs[3:3 + n_gain]
    out_refs = refs[3 + n_gain + n_carried:]
    n = _normed_rows(x_ref[...], g_ref[...])
    o = 0
    for col0, width, head_dim, gain_idx, scale, n_outs in segments:
        for c in range(0, width, group):
            y = _dot(n, w_ref[:, col0 + c:col0 + c + group])
            for h0 in range(0, group, head_dim):
                yh = y[:, h0:h0 + head_dim]
                if gain_idx is not None:
                    r = lax.rsqrt(jnp.mean(yh * yh, axis=-1, keepdims=True) + EPS)
                    yh = yh * r * gain_refs[gain_idx][...]
                if scale != 1.0:
                    yh = yh * scale
                for t in range(n_outs):
                    ref = out_refs[o + t]
                    if len(ref.shape) == 3:
                        ref[:, (c + h0) // head_dim, :] = yh.astype(ref.dtype)
                    else:
                        ref[:, c + h0:c + h0 + head_dim] = yh.astype(ref.dtype)
        o += n_outs


def _proj(x, g, w, segments, gains, out_dtypes, cache_slot=None, tail_of=None):
    n, d = x.shape
    tm = min(TOKEN_TILE, n)
    group = 512
    segs = tuple((c0, wd, hd, gi, sc, len(out_dtypes[s])) for s, (c0, wd, hd, gi, sc) in enumerate(segments))
    out_shape, out_specs, cache_outs = [], [], []
    for (c0, wd, hd, gi, sc), dts in zip(segments, out_dtypes):
        for dt in dts:
            if tail_of is not None and dt == F32:
                out_shape.append(jax.ShapeDtypeStruct((n // tail_of * tm, wd), dt))
                out_specs.append(pl.BlockSpec((tm, wd), functools.partial(lambda i, s: (i // s, 0), s=tail_of // tm)))
            elif cache_slot is not None and dt == F32:
                slot, n_slots, _ = cache_slot
                cache_outs.append(len(out_shape))
                out_shape.append(jax.ShapeDtypeStruct((n_slots, n, wd // hd, hd), dt))
                out_specs.append(pl.BlockSpec((None, tm, wd // hd, hd),
                                              functools.partial(lambda i, s: (s, i, 0, 0), s=slot)))
            else:
                out_shape.append(jax.ShapeDtypeStruct((n, wd), dt))
                out_specs.append(pl.BlockSpec((tm, wd), lambda i: (i, 0)))
    carried = [] if cache_slot is None or cache_slot[2] is None else list(cache_slot[2])
    n_fixed = 3 + len(gains)
    in_specs = [
        pl.BlockSpec((tm, d), lambda i: (i, 0)),
        pl.BlockSpec((1, d), lambda i: (0, 0)),
        pl.BlockSpec(w.shape, lambda i: (0, 0), pipeline_mode=pl.Buffered(1)),
    ] + [pl.BlockSpec((1, gn.shape[-1]), lambda i: (0, 0)) for gn in gains] \
      + [pl.BlockSpec(memory_space=pl.ANY) for _ in carried]
    return pl.pallas_call(
        functools.partial(_proj_kernel, segments=segs, group=group, n_gain=len(gains), n_carried=len(carried)),
        grid=(n // tm,),
        in_specs=in_specs,
        out_specs=out_specs,
        out_shape=out_shape,
        input_output_aliases={n_fixed + c: cache_outs[c] for c in range(len(carried))},
        compiler_params=_params("arbitrary" if tail_of is not None else "parallel"),
        name="proj",
    )(x, g.reshape(1, d), w, *[gn.reshape(1, -1) for gn in gains], *carried)


def _band_kernel(q_ref, k0_ref, k1_ref, k2_ref, v0_ref, v1_ref, v2_ref, bias_ref, o_ref):
    tq = q_ref.shape[0]
    i = pl.program_id(1)
    col = lax.broadcasted_iota(jnp.int32, (tq, 3 * tq), 1)
    in_seq = col >= (2 - i) * tq
    for h in range(A_HEADS):
        sl = slice(h * A_DIM, (h + 1) * A_DIM)
        kh = jnp.concatenate([k0_ref[:, sl], k1_ref[:, sl], k2_ref[:, sl]], axis=0)
        vh = jnp.concatenate([v0_ref[:, sl], v1_ref[:, sl], v2_ref[:, sl]], axis=0)
        s = _dot_t(q_ref[:, sl], kh) + bias_ref[h]
        s = jnp.where(in_seq, s, NEG_INF)
        p = jnp.exp(s - jnp.max(s, axis=-1, keepdims=True))
        l = jnp.sum(p, axis=-1, keepdims=True)
        o_ref[:, sl] = (_dot(p.astype(BF16), vh) / l).astype(o_ref.dtype)


def _band_table(rel_bias, q0, nq, k0, nk):
    heads = rel_bias.shape[0]
    length = nq + nk - 1
    dist = (q0 - k0) - (nk - 1) + np.arange(length)
    n_left = int(np.sum(dist < -REL_CLIP))
    n_right = int(np.sum(dist > REL_CLIP))
    parts = [jnp.broadcast_to(rel_bias[:, :1], (heads, n_left))]
    if n_left + n_right < length:
        lo = int(dist[n_left]) + REL_CLIP
        hi = int(dist[length - n_right - 1]) + REL_CLIP
        parts.append(rel_bias[:, lo:hi + 1])
    parts.append(jnp.broadcast_to(rel_bias[:, -1:], (heads, n_right + 1)))
    e = jnp.concatenate(parts, axis=1).astype(F32)
    period = length + 1
    rows = jnp.tile(e, (1, nq + 1))[:, :nq * (period + 1)].reshape(heads, nq, period + 1)
    table = rows[:, :, :nk][:, :, ::-1]
    q_pos = q0 + np.arange(nq)
    k_pos = k0 + np.arange(nk)
    qc = q_pos // CHUNK
    kc = k_pos // CHUNK
    valid = (k_pos[None, :] >= 0) & (kc[None, :] <= qc[:, None]) & (kc[None, :] >= qc[:, None] - LEFT_CHUNKS)
    return jnp.where(jnp.asarray(valid)[None], table, NEG_INF)


def _band_prompt(q, k, v, rel_bias):
    b, t, w = q.shape
    tq = BAND_Q_TILE
    bias = _band_table(rel_bias, 2 * tq, tq, 0, 3 * tq)
    qspec = pl.BlockSpec((None, tq, w), lambda bi, i: (bi, i, 0))
    kspecs = [pl.BlockSpec((None, tq, w), functools.partial(lambda bi, i, d: (bi, jnp.maximum(i - d, 0), 0), d=d))
              for d in (2, 1, 0)]
    return pl.pallas_call(
        _band_kernel,
        grid=(b, t // tq),
        in_specs=[qspec] + kspecs + kspecs + [
            pl.BlockSpec(bias.shape, lambda bi, i: (0, 0, 0), pipeline_mode=pl.Buffered(1))],
        out_specs=qspec,
        out_shape=jax.ShapeDtypeStruct((b, t, w), BF16),
        compiler_params=_params("parallel", "parallel"),
        name="band_prompt",
    )(q, k, k, k, v, v, v, bias)


def _band_step_kernel(q_ref, ck_ref, cv_ref, nk_ref, nv_ref, bc_ref, bn_ref, o_ref):
    for h in range(A_HEADS):
        sl = slice(h * A_DIM, (h + 1) * A_DIM)
        qh = q_ref[:, sl]
        sc = _dot_t(qh, ck_ref[:, h, :].astype(BF16)) + bc_ref[h]
        sn = _dot_t(qh, nk_ref[:, sl]) + bn_ref[h]
        m = jnp.maximum(jnp.max(sc, axis=-1, keepdims=True), jnp.max(sn, axis=-1, keepdims=True))
        pc = jnp.exp(sc - m)
        pn = jnp.exp(sn - m)
        l = jnp.sum(pc, axis=-1, keepdims=True) + jnp.sum(pn, axis=-1, keepdims=True)
        o = _dot(pc.astype(BF16), cv_ref[:, h, :].astype(BF16)) + _dot(pn.astype(BF16), nv_ref[:, sl])
        o_ref[:, sl] = (o / l).astype(o_ref.dtype)


def _band_step(q, ck, cv, layer, nk, nv, rel_bias, past):
    n, ts, w = q.shape
    c = ck.shape[2]
    bias_c = _band_table(rel_bias, past, ts, past - c, c)
    bias_n = _band_table(rel_bias, past, ts, past, ts)
    new = pl.BlockSpec((None, ts, w), lambda i: (i, 0, 0))
    old = pl.BlockSpec((None, None, c, A_HEADS, A_DIM), lambda i: (layer, i, 0, 0, 0))
    return pl.pallas_call(
        _band_step_kernel,
        grid=(n,),
        in_specs=[new, old, old, new, new,
                  pl.BlockSpec(bias_c.shape, lambda i: (0, 0, 0)),
                  pl.BlockSpec(bias_n.shape, lambda i: (0, 0, 0))],
        out_specs=new,
        out_shape=jax.ShapeDtypeStruct((n, ts, w), BF16),
        compiler_params=_params("parallel"),
        name="band_step",
    )(q, ck, cv, nk, nv, bias_c, bias_n)


def _suffix_ones(n):
    return (jnp.arange(n)[:, None] >= jnp.arange(n)[None, :]).astype(BF16)


def _sb_block(q, kb, vb, u, aft, mask):
    z = _dot_t(q, kb)
    if mask is not None:
        z = jnp.where(mask, z, SB_NEG)
    loc = _dot(_softplus(z).astype(BF16), u)
    w = jnp.exp(z - loc - aft)
    return _dot(w.astype(BF16), vb), aft + loc[:, 0:1]


def _sb_kernel(q_ref, k_ref, v_ref, u_ref, o_ref, acc_ref, aft_ref):
    t = q_ref.shape[0]
    i = pl.program_id(2)
    q = q_ref[...]
    u = u_ref[...]
    causal = lax.broadcasted_iota(jnp.int32, (t, t), 1) < lax.broadcasted_iota(jnp.int32, (t, t), 0)

    def block(j):
        rows = pl.ds(pl.multiple_of(j * t, t), t)
        return k_ref[rows, :], v_ref[rows, :]

    kd, vd = block(i)
    out_d, aft = _sb_block(q, kd, vd, u, 0.0, causal)
    kp, vp = block(jnp.maximum(i - 1, 0))
    out_p, aft = _sb_block(q, kp, vp, u, aft, i > 0)
    acc_ref[...] = out_d + out_p
    aft_ref[...] = aft

    def more(carry):
        j, least_aft = carry
        return jnp.logical_and(j >= 0, least_aft < SB_EXIT_MASS)

    def older(carry):
        j, _ = carry
        kb, vb = block(j)
        out, aft = _sb_block(q, kb, vb, u, aft_ref[...], None)
        acc_ref[...] += out
        aft_ref[...] = aft
        return j - 1, jnp.min(aft)

    lax.while_loop(more, older, (i - 2, jnp.min(aft)))
    o_ref[...] = acc_ref[...].astype(o_ref.dtype)


def _sb_prompt(q, k, v):
    b, t, w = q.shape
    tq = min(SB_TILE, t)
    qspec = pl.BlockSpec((None, tq, B_DIM), lambda bi, h, i: (bi, i, h))
    kspec = pl.BlockSpec((None, t, B_DIM), lambda bi, h, i: (bi, 0, h))
    return pl.pallas_call(
        _sb_kernel,
        grid=(b, B_HEADS, t // tq),
        in_specs=[qspec, kspec, kspec, pl.BlockSpec((tq, tq), lambda bi, h, i: (0, 0))],
        out_specs=qspec,
        out_shape=jax.ShapeDtypeStruct((b, t, w), BF16),
        scratch_shapes=[pltpu.VMEM((tq, B_DIM), F32), pltpu.VMEM((tq, 1), F32)],
        compiler_params=_params("parallel", "parallel", "arbitrary"),
        name="sb_prompt",
    )(q, k, v, _suffix_ones(tq))


def _sb_step_kernel(q_ref, ck_ref, cv_ref, nk_ref, nv_ref, un_ref, uc_ref, o_ref):
    ts = q_ref.shape[0]
    tk = uc_ref.shape[0]
    n_cached = ck_ref.shape[0] // tk
    causal = lax.broadcasted_iota(jnp.int32, (ts, ts), 1) < lax.broadcasted_iota(jnp.int32, (ts, ts), 0)
    for h in range(B_HEADS):
        sl = slice(h * B_DIM, (h + 1) * B_DIM)
        q = q_ref[:, sl]
        acc, aft = _sb_block(q, nk_ref[:, sl], nv_ref[:, sl], un_ref[...], 0.0, causal)
        for j in reversed(range(n_cached)):
            rows = slice(j * tk, (j + 1) * tk)
            out, aft = _sb_block(q, ck_ref[rows, h, :].astype(BF16), cv_ref[rows, h, :].astype(BF16),
                                 uc_ref[...], aft, None)
            acc = acc + out
        o_ref[:, sl] = acc.astype(o_ref.dtype)


def _sb_step(q, ck, cv, layer, nk, nv):
    n, ts, w = q.shape
    p = ck.shape[2]
    tk = min(SB_TILE, p)
    new = pl.BlockSpec((None, ts, w), lambda i: (i, 0, 0))
    old = pl.BlockSpec((None, None, p, B_HEADS, B_DIM), lambda i: (layer, i, 0, 0, 0))
    return pl.pallas_call(
        _sb_step_kernel,
        grid=(n,),
        in_specs=[new, old, old, new, new,
                  pl.BlockSpec((ts, ts), lambda i: (0, 0)),
                  pl.BlockSpec((tk, tk), lambda i: (0, 0))],
        out_specs=new,
        out_shape=jax.ShapeDtypeStruct((n, ts, w), BF16),
        compiler_params=_params("parallel"),
        name="sb_step",
    )(q, ck, cv, nk, nv, _suffix_ones(ts), _suffix_ones(tk))


def _merge_kernel(x_ref, tok_ref, qm_ref, mk_ref, mv_ref, w_ref, o_ref, mo_ref):
    split_heads = len(mk_ref.shape) == 3
    for h in range(N_MEM_HEADS):
        sl = slice(h * MEM_DIM, (h + 1) * MEM_DIM)
        mk = mk_ref[:, h, :] if split_heads else mk_ref[:, sl]
        mv = mv_ref[:, h, :] if split_heads else mv_ref[:, sl]
        s = _dot_t(qm_ref[:, sl], mk.astype(BF16))
        p = jnp.exp(s - jnp.max(s, axis=-1, keepdims=True))
        l = jnp.sum(p, axis=-1, keepdims=True)
        mo_ref[:, sl] = (_dot(p.astype(BF16), mv.astype(BF16)) / l).astype(BF16)
    y = _dot(tok_ref[...], w_ref[:TOK_W, :]) + _dot(mo_ref[...], w_ref[TOK_W:, :])
    o_ref[...] = x_ref[...] + y


def _merge(x, tok, qm, mk, mv, w_out):
    b, t, d = x.shape
    tm = min(TOKEN_TILE, t)
    row = lambda width: pl.BlockSpec((None, tm, width), lambda bi, i: (bi, i, 0))
    mem = pl.BlockSpec((None,) + mk.shape[1:], lambda bi, i: (bi,) + (0,) * (mk.ndim - 1))
    return pl.pallas_call(
        _merge_kernel,
        grid=(b, t // tm),
        in_specs=[row(d), row(TOK_W), row(MEM_W), mem, mem,
                  pl.BlockSpec(w_out.shape, lambda bi, i: (0, 0), pipeline_mode=pl.Buffered(1))],
        out_specs=row(d),
        out_shape=jax.ShapeDtypeStruct((b, t, d), F32),
        scratch_shapes=[pltpu.VMEM((tm, MEM_W), BF16)],
        compiler_params=_params("parallel", "parallel"),
        name="merge",
    )(x, tok, qm, mk, mv, w_out)


def kernel(x_prompt, x_sample, mem_prompt, cache_a_k, cache_a_v, cache_b_k, cache_b_v, cache_mem_k, cache_mem_v, ffn1_norm, ffn1_w_in, ffn1_w_out, attn_norm, w_in, w_out, a_q_gain, a_k_gain, a_rel_bias, mem_norm, w_mem_kv, mem_q_gain, mem_k_gain, ffn2_norm, ffn2_w_in, ffn2_w_out):
    depth = w_in.shape[0]
    bp, tp, d = x_prompt.shape
    bs, ts, _ = x_sample.shape
    n_mem = mem_prompt.shape[1]
    past = cache_b_k.shape[2]
    a_keep = min(BAND_PAST, tp)

    ffn1_w_in, ffn1_w_out, ffn2_w_in, ffn2_w_out, w_in, w_out, w_mem_kv = (
        w.astype(BF16) for w in (ffn1_w_in, ffn1_w_out, ffn2_w_in, ffn2_w_out, w_in, w_out, w_mem_kv))

    x_p = x_prompt.reshape(bp * tp, d)
    x_s = x_sample.reshape(bs * ts, d)
    mem = mem_prompt.reshape(bp * n_mem, d)
    assert a_keep == min(TOKEN_TILE, tp), "band cache rows must be exactly the last token tile of a sequence"
    a_k_p, a_v_p, m_k_p, m_v_p = [], [], [], []
    n_layers = {True: (depth + 1) // 2, False: depth // 2}
    new_kv_p = None
    new_kv_s = {True: None, False: None}
    for l in range(depth):
        j = l // 2
        band = l % 2 == 0
        heads, dim = (A_HEADS, A_DIM) if band else (B_HEADS, B_DIM)
        x_p = _ffn(x_p, ffn1_norm[l], ffn1_w_in[l], ffn1_w_out[l])
        x_s = _ffn(x_s, ffn1_norm[l], ffn1_w_in[l], ffn1_w_out[l])

        segments = [(0, TOK_W, dim, 0 if band else None, dim ** -0.5),
                    (TOK_W, TOK_W, dim, 1 if band else None, 1.0),
                    (2 * TOK_W, TOK_W, dim, None, 1.0),
                    (3 * TOK_W, MEM_W, MEM_DIM, 2 if band else 0, MEM_DIM ** -0.5)]
        gains = [a_q_gain[j], a_k_gain[j], mem_q_gain[l]] if band else [mem_q_gain[l]]
        dtypes = [[BF16], [BF16, F32], [BF16, F32], [BF16]]
        if band:
            q_p, k_p, kf_p, v_p, vf_p, qm_p = _proj(x_p, attn_norm[l], w_in[l], segments, gains, dtypes, tail_of=tp)
            a_k_p.append(kf_p.reshape(bp, a_keep, A_HEADS, A_DIM))
            a_v_p.append(vf_p.reshape(bp, a_keep, A_HEADS, A_DIM))
        else:
            q_p, k_p, kf_p, v_p, vf_p, qm_p = _proj(x_p, attn_norm[l], w_in[l], segments, gains, dtypes,
                                                    (j, n_layers[band], new_kv_p))
            new_kv_p = [kf_p, vf_p]
        q_s, k_s, kf_s, v_s, vf_s, qm_s = _proj(x_s, attn_norm[l], w_in[l], segments, gains, dtypes,
                                                (j, n_layers[band], new_kv_s[band]))
        new_kv_s[band] = [kf_s, vf_s]
        in_p = lambda a: a.reshape(bp, tp, -1)
        in_s = lambda a: a.reshape(bs, ts, -1)

        if band:
            tok_p = _band_prompt(in_p(q_p), in_p(k_p), in_p(v_p), a_rel_bias[j])
            tok_s = _band_step(in_s(q_s), cache_a_k, cache_a_v, j, in_s(k_s), in_s(v_s), a_rel_bias[j], past)
        else:
            tok_p = _sb_prompt(in_p(q_p), in_p(k_p), in_p(v_p))
            tok_s = _sb_step(in_s(q_s), cache_b_k, cache_b_v, j, in_s(k_s), in_s(v_s))

        mem_segments = [(0, MEM_W, MEM_DIM, 0, 1.0), (MEM_W, MEM_W, MEM_DIM, None, 1.0)]
        mk, mkf, mv, mvf = _proj(mem, mem_norm[l], w_mem_kv[l], mem_segments, [mem_k_gain[l]],
                                 [[BF16, F32], [BF16, F32]])
        m_k_p.append(mkf.reshape(bp, n_mem, N_MEM_HEADS, MEM_DIM))
        m_v_p.append(mvf.reshape(bp, n_mem, N_MEM_HEADS, MEM_DIM))

        x_p = _merge(in_p(x_p), tok_p, in_p(qm_p), mk.reshape(bp, n_mem, MEM_W), mv.reshape(bp, n_mem, MEM_W),
                     w_out[l]).reshape(bp * tp, d)
        x_s = _merge(in_s(x_s), tok_s, in_s(qm_s), cache_mem_k[l], cache_mem_v[l], w_out[l]).reshape(bs * ts, d)

        x_p = _ffn(x_p, ffn2_norm[l], ffn2_w_in[l], ffn2_w_out[l])
        x_s = _ffn(x_s, ffn2_norm[l], ffn2_w_in[l], ffn2_w_out[l])

    b_k_p, b_v_p = (c.reshape(-1, bp, tp, B_HEADS, B_DIM) for c in new_kv_p)
    a_k_s, a_v_s = (c.reshape(-1, bs, ts, A_HEADS, A_DIM) for c in new_kv_s[True])
    b_k_s, b_v_s = (c.reshape(-1, bs, ts, B_HEADS, B_DIM) for c in new_kv_s[False])
    return (x_p.reshape(bp, tp, d), x_s.reshape(bs, ts, d), jnp.stack(a_k_p), jnp.stack(a_v_p), b_k_p, b_v_p,
            jnp.stack(m_k_p), jnp.stack(m_v_p), a_k_s, a_v_s, b_k_s, b_v_s)
```

```python
import functools

import jax
import jax.numpy as jnp
import numpy as np
from jax import lax
from jax.experimental import pallas as pl
from jax.experimental.pallas import tpu as pltpu

CHUNK = 64
LEFT_CHUNKS = 8
BAND_PAST = LEFT_CHUNKS * CHUNK
REL_CLIP = 128
TOK_W = 1024
A_HEADS, A_DIM = 8, 128
B_HEADS, B_DIM = 4, 256
N_MEM_HEADS, MEM_DIM = 4, 128
MEM_W = N_MEM_HEADS * MEM_DIM
EPS = 1e-6
NEG_INF = -1e30
SB_NEG = -1e4

V7X_VMEM_LIMIT_BYTES = 56 * 1024 * 1024
TOKEN_TILE = 512
FF_TILE = 1024
BF16_ROWS = 16
BAND_Q_TILE = BAND_PAST // 2
SB_TILE = 256
SB_EXIT_MASS = 120.0

BF16 = jnp.bfloat16
F32 = jnp.float32


def _params(*semantics):
    return pltpu.CompilerParams(dimension_semantics=semantics, vmem_limit_bytes=V7X_VMEM_LIMIT_BYTES)


def _dot(a, b):
    return jnp.dot(a, b, preferred_element_type=F32)


def _dot_t(a, b):
    return lax.dot_general(a, b, (((1,), (1,)), ((), ())), preferred_element_type=F32)


def _normed_rows(x, g):
    r = lax.rsqrt(jnp.mean(x * x, axis=-1, keepdims=True) + EPS)
    return (x * r * g).astype(BF16)


def _softplus(z):
    return jnp.log1p(jnp.exp(z))


def _ffn_kernel(x_ref, g_ref, wa_ref, wb_ref, wo_ref, o_ref, n_ref, x_keep_ref, *, nf, part_rows):
    i = pl.program_id(0)
    j = pl.program_id(1)
    slot = i % 2

    @pl.when(jnp.logical_and(i == 0, j == 0))
    def _():
        n_ref[0] = _normed_rows(x_ref[...], g_ref[...])
        x_keep_ref[...] = x_ref[...]

    @pl.when(i > 0)
    def _():
        @pl.when(j == 0)
        def _():
            o_ref[...] = x_keep_ref[...]

        n = n_ref[1 - slot]
        a = _dot(n, wa_ref[...])
        b = _dot(n, wb_ref[...])
        h = (a / (1.0 + jnp.exp(-a)) * b * 0.5).astype(BF16)
        o_ref[...] += _dot(h, wo_ref[...])

        start = jnp.minimum(j * part_rows, x_ref.shape[0] - part_rows)
        part = pl.ds(pl.multiple_of(start, BF16_ROWS), part_rows)
        rows = x_ref[part, :]
        n_ref[slot, part, :] = _normed_rows(rows, g_ref[...])
        x_keep_ref[part, :] = rows


def _ffn(x, g, w_in, w_out, layer):
    n, d = x.shape
    f = w_out.shape[1]
    tm = min(TOKEN_TILE, n)
    tf = min(FF_TILE, f)
    nf = f // tf
    n_tiles = n // tm
    part_rows = -(-tm // (nf * BF16_ROWS)) * BF16_ROWS
    hidden = lambda i, j: jnp.where(i == 0, 0, j)
    return pl.pallas_call(
        functools.partial(_ffn_kernel, nf=nf, part_rows=part_rows),
        grid=(n_tiles + 1, nf),
        in_specs=[
            pl.BlockSpec((tm, d), lambda i, j: (jnp.minimum(i, n_tiles - 1), 0)),
            pl.BlockSpec((1, d), lambda i, j: (0, 0)),
            pl.BlockSpec((None, d, tf), lambda i, j: (layer, 0, hidden(i, j))),
            pl.BlockSpec((None, d, tf), lambda i, j: (layer, 0, hidden(i, j) + nf)),
            pl.BlockSpec((None, tf, d), lambda i, j: (layer, hidden(i, j), 0)),
        ],
        out_specs=pl.BlockSpec((tm, d), lambda i, j: (jnp.maximum(i - 1, 0), 0)),
        out_shape=jax.ShapeDtypeStruct((n, d), F32),
        scratch_shapes=[pltpu.VMEM((2, tm, d), BF16), pltpu.VMEM((tm, d), F32)],
        compiler_params=_params("arbitrary", "arbitrary"),
        name="ffn",
    )(x, g.reshape(1, d), w_in, w_in, w_out)


def _proj_kernel(*refs, segments, group, n_gain, n_carried):
    x_ref, g_ref, w_ref = refs[:3]
    gain_refs = refs[3:3 + n_gain]
    out_refs = refs[3 + n_gain + n_carried:]
    n = _normed_rows(x_ref[...], g_ref[...])
    o = 0
    for col0, width, head_dim, gain_idx, scale, n_outs in segments:
        for c in range(0, width, group):
            y = _dot(n, w_ref[:, col0 + c:col0 + c + group])
            for h0 in range(0, group, head_dim):
                yh = y[:, h0:h0 + head_dim]
                if gain_idx is not None:
                    r = lax.rsqrt(jnp.mean(yh * yh, axis=-1, keepdims=True) + EPS)
                    yh = yh * r * gain_refs[gain_idx][...]
                if scale != 1.0:
                    yh = yh * scale
                for t in range(n_outs):
                    ref = out_refs[o + t]
                    if len(ref.shape) == 3:
                        ref[:, (c + h0) // head_dim, :] = yh.astype(ref.dtype)
                    else:
                        ref[:, c + h0:c + h0 + head_dim] = yh.astype(ref.dtype)
        o += n_outs


def _proj(x, g, w, layer, segments, gains, out_dtypes, cache_slot=None, tail_of=None):
    n, d = x.shape
    tm = min(TOKEN_TILE, n)
    group = 512
    segs = tuple((c0, wd, hd, gi, sc, len(out_dtypes[s])) for s, (c0, wd, hd, gi, sc) in enumerate(segments))
    out_shape, out_specs, cache_outs = [], [], []
    for (c0, wd, hd, gi, sc), dts in zip(segments, out_dtypes):
        for dt in dts:
            if tail_of is not None and dt == F32:
                out_shape.append(jax.ShapeDtypeStruct((n // tail_of * tm, wd), dt))
                out_specs.append(pl.BlockSpec((tm, wd), functools.partial(lambda i, s: (i // s, 0), s=tail_of // tm)))
            elif cache_slot is not None and dt == F32:
                slot, n_slots, _ = cache_slot
                cache_outs.append(len(out_shape))
                out_shape.append(jax.ShapeDtypeStruct((n_slots, n, wd // hd, hd), dt))
                out_specs.append(pl.BlockSpec((None, tm, wd // hd, hd),
                                              functools.partial(lambda i, s: (s, i, 0, 0), s=slot)))
            else:
                out_shape.append(jax.ShapeDtypeStruct((n, wd), dt))
                out_specs.append(pl.BlockSpec((tm, wd), lambda i: (i, 0)))
    carried = [] if cache_slot is None or cache_slot[2] is None else list(cache_slot[2])
    n_fixed = 3 + len(gains)
    in_specs = [
        pl.BlockSpec((tm, d), lambda i: (i, 0)),
        pl.BlockSpec((1, d), lambda i: (0, 0)),
        pl.BlockSpec((None,) + w.shape[1:], lambda i: (layer, 0, 0), pipeline_mode=pl.Buffered(1)),
    ] + [pl.BlockSpec((1, gn.shape[-1]), lambda i: (0, 0)) for gn in gains] \
      + [pl.BlockSpec(memory_space=pl.ANY) for _ in carried]
    return pl.pallas_call(
        functools.partial(_proj_kernel, segments=segs, group=group, n_gain=len(gains), n_carried=len(carried)),
        grid=(n // tm,),
        in_specs=in_specs,
        out_specs=out_specs,
        out_shape=out_shape,
        input_output_aliases={n_fixed + c: cache_outs[c] for c in range(len(carried))},
        compiler_params=_params("arbitrary" if tail_of is not None else "parallel"),
        name="proj",
    )(x, g.reshape(1, d), w, *[gn.reshape(1, -1) for gn in gains], *carried)


def _band_kernel(q_ref, k0_ref, k1_ref, k2_ref, v0_ref, v1_ref, v2_ref, bias_ref, o_ref):
    tq = q_ref.shape[0]
    i = pl.program_id(1)
    col = lax.broadcasted_iota(jnp.int32, (tq, 3 * tq), 1)
    in_seq = col >= (2 - i) * tq
    for h in range(A_HEADS):
        sl = slice(h * A_DIM, (h + 1) * A_DIM)
        kh = jnp.concatenate([k0_ref[:, sl], k1_ref[:, sl], k2_ref[:, sl]], axis=0)
        vh = jnp.concatenate([v0_ref[:, sl], v1_ref[:, sl], v2_ref[:, sl]], axis=0)
        s = _dot_t(q_ref[:, sl], kh) + bias_ref[h]
        s = jnp.where(in_seq, s, NEG_INF)
        p = jnp.exp(s - jnp.max(s, axis=-1, keepdims=True))
        l = jnp.sum(p, axis=-1, keepdims=True)
        o_ref[:, sl] = (_dot(p.astype(BF16), vh) / l).astype(o_ref.dtype)


def _band_table(rel_bias, q0, nq, k0, nk):
    heads = rel_bias.shape[0]
    length = nq + nk - 1
    dist = (q0 - k0) + (nq - 1) - np.arange(length)
    n_far = int(np.sum(dist > REL_CLIP))
    n_ahead = int(np.sum(dist < -REL_CLIP))
    parts = [jnp.broadcast_to(rel_bias[:, -1:], (heads, n_far))]
    if n_far + n_ahead < length:
        hi = int(dist[n_far]) + REL_CLIP
        lo = int(dist[length - n_ahead - 1]) + REL_CLIP
        parts.append(rel_bias[:, lo:hi + 1][:, ::-1])
    parts.append(jnp.broadcast_to(rel_bias[:, :1], (heads, n_ahead + 1)))
    e = jnp.concatenate(parts, axis=1).astype(F32)
    period = length + 1
    rows = jnp.tile(e, (1, nq))[:, :nq * (period - 1)].reshape(heads, nq, period - 1)
    table = rows[:, :, nq - 1:nq - 1 + nk]
    q_pos = q0 + np.arange(nq)
    k_pos = k0 + np.arange(nk)
    qc = q_pos // CHUNK
    kc = k_pos // CHUNK
    valid = (k_pos[None, :] >= 0) & (kc[None, :] <= qc[:, None]) & (kc[None, :] >= qc[:, None] - LEFT_CHUNKS)
    return jnp.where(jnp.asarray(valid)[None], table, NEG_INF)


def _band_prompt(q, k, v, rel_bias):
    b, t, w = q.shape
    tq = BAND_Q_TILE
    bias = _band_table(rel_bias, 2 * tq, tq, 0, 3 * tq)
    qspec = pl.BlockSpec((None, tq, w), lambda bi, i: (bi, i, 0))
    kspecs = [pl.BlockSpec((None, tq, w), functools.partial(lambda bi, i, d: (bi, jnp.maximum(i - d, 0), 0), d=d))
              for d in (2, 1, 0)]
    return pl.pallas_call(
        _band_kernel,
        grid=(b, t // tq),
        in_specs=[qspec] + kspecs + kspecs + [
            pl.BlockSpec(bias.shape, lambda bi, i: (0, 0, 0), pipeline_mode=pl.Buffered(1))],
        out_specs=qspec,
        out_shape=jax.ShapeDtypeStruct((b, t, w), BF16),
        compiler_params=_params("parallel", "parallel"),
        name="band_prompt",
    )(q, k, k, k, v, v, v, bias)


def _band_step_kernel(q_ref, ck_ref, cv_ref, nk_ref, nv_ref, bc_ref, bn_ref, o_ref):
    ts = q_ref.shape[0]
    frames, heads, dim = ck_ref.shape
    head_cols = [slice(h * dim, (h + 1) * dim) for h in range(heads)]
    q = jnp.concatenate([q_ref[:, sl] for sl in head_cols], axis=0)
    sc = _dot_t(q, ck_ref[...].reshape(frames * heads, dim).astype(BF16)) + bc_ref[...]
    sn = jnp.concatenate([_dot_t(q_ref[:, sl], nk_ref[:, sl]) + bn_ref[h] for h, sl in enumerate(head_cols)], axis=0)
    m = jnp.maximum(jnp.max(sc, axis=-1, keepdims=True), jnp.max(sn, axis=-1, keepdims=True))
    pc = jnp.exp(sc - m)
    pn = jnp.exp(sn - m)
    l = jnp.sum(pc, axis=-1, keepdims=True) + jnp.sum(pn, axis=-1, keepdims=True)
    o = _dot(pc.astype(BF16), cv_ref[...].reshape(frames * heads, dim).astype(BF16))
    o = o + jnp.concatenate([_dot(pn[h * ts:(h + 1) * ts].astype(BF16), nv_ref[:, sl])
                             for h, sl in enumerate(head_cols)], axis=0)
    o = o / l
    for h, sl in enumerate(head_cols):
        o_ref[:, sl] = o[h * ts:(h + 1) * ts].astype(o_ref.dtype)


def _band_step(q, ck, cv, layer, nk, nv, rel_bias, past):
    n, ts, w = q.shape
    c = ck.shape[2]
    bias_c = _band_table(rel_bias, past, ts, past - c, c)
    bias_n = _band_table(rel_bias, past, ts, past, ts)
    same_head = jnp.eye(A_HEADS, dtype=bool)[:, None, None, :]
    bias_c = jnp.where(same_head, bias_c[:, :, :, None], NEG_INF).reshape(A_HEADS * ts, c * A_HEADS)
    new = pl.BlockSpec((None, ts, w), lambda i: (i, 0, 0))
    old = pl.BlockSpec((None, None, c, A_HEADS, A_DIM), lambda i: (layer, i, 0, 0, 0))
    return pl.pallas_call(
        _band_step_kernel,
        grid=(n,),
        in_specs=[new, old, old, new, new,
                  pl.BlockSpec(bias_c.shape, lambda i: (0, 0)),
                  pl.BlockSpec(bias_n.shape, lambda i: (0, 0, 0))],
        out_specs=new,
        out_shape=jax.ShapeDtypeStruct((n, ts, w), BF16),
        compiler_params=_params("parallel"),
        name="band_step",
    )(q, ck, cv, nk, nv, bias_c, bias_n)


def _suffix_ones(n):
    return (jnp.arange(n)[:, None] >= jnp.arange(n)[None, :]).astype(BF16)


def _sb_block(q, kb, vb, u, aft, mask):
    z = _dot_t(q, kb)
    if mask is not None:
        z = jnp.where(mask, z, SB_NEG)
    loc = _dot(_softplus(z).astype(BF16), u)
    w = jnp.exp(z - loc - aft)
    return _dot(w.astype(BF16), vb), aft + loc[:, 0:1]


def _sb_kernel(q_ref, k_ref, v_ref, u_ref, o_ref, acc_ref, aft_ref):
    t = q_ref.shape[0]
    i = pl.program_id(2)
    q = q_ref[...]
    u = u_ref[...]
    causal = lax.broadcasted_iota(jnp.int32, (t, t), 1) < lax.broadcasted_iota(jnp.int32, (t, t), 0)

    def block(j):
        rows = pl.ds(pl.multiple_of(j * t, t), t)
        return k_ref[rows, :], v_ref[rows, :]

    kd, vd = block(i)
    out_d, aft = _sb_block(q, kd, vd, u, 0.0, causal)
    kp, vp = block(jnp.maximum(i - 1, 0))
    out_p, aft = _sb_block(q, kp, vp, u, aft, i > 0)
    acc_ref[...] = out_d + out_p
    aft_ref[...] = aft

    def more(carry):
        j, least_aft = carry
        return jnp.logical_and(j >= 0, least_aft < SB_EXIT_MASS)

    def older(carry):
        j, _ = carry
        kb, vb = block(j)
        out, aft = _sb_block(q, kb, vb, u, aft_ref[...], None)
        acc_ref[...] += out
        aft_ref[...] = aft
        return j - 1, jnp.min(aft)

    lax.while_loop(more, older, (i - 2, jnp.min(aft)))
    o_ref[...] = acc_ref[...].astype(o_ref.dtype)


def _sb_prompt(q, k, v):
    b, t, w = q.shape
    tq = min(SB_TILE, t)
    qspec = pl.BlockSpec((None, tq, B_DIM), lambda bi, h, i: (bi, i, h))
    kspec = pl.BlockSpec((None, t, B_DIM), lambda bi, h, i: (bi, 0, h))
    return pl.pallas_call(
        _sb_kernel,
        grid=(b, B_HEADS, t // tq),
        in_specs=[qspec, kspec, kspec, pl.BlockSpec((tq, tq), lambda bi, h, i: (0, 0))],
        out_specs=qspec,
        out_shape=jax.ShapeDtypeStruct((b, t, w), BF16),
        scratch_shapes=[pltpu.VMEM((tq, B_DIM), F32), pltpu.VMEM((tq, 1), F32)],
        compiler_params=_params("parallel", "parallel", "arbitrary"),
        name="sb_prompt",
    )(q, k, v, _suffix_ones(tq))


def _sb_step_kernel(q_ref, ck_ref, cv_ref, nk_ref, nv_ref, un_ref, uc_ref, o_ref):
    ts = q_ref.shape[0]
    tk = uc_ref.shape[0]
    frames, heads, dim = ck_ref.shape
    causal = lax.broadcasted_iota(jnp.int32, (ts, ts), 1) < lax.broadcasted_iota(jnp.int32, (ts, ts), 0)
    head_cols = [slice(h * dim, (h + 1) * dim) for h in range(heads)]
    own = [_sb_block(q_ref[:, sl], nk_ref[:, sl], nv_ref[:, sl], un_ref[...], 0.0, causal) for sl in head_cols]
    q = jnp.concatenate([q_ref[:, sl] for sl in head_cols], axis=0)
    acc = jnp.concatenate([o for o, _ in own], axis=0)
    aft = jnp.concatenate([a for _, a in own], axis=0)
    own_head = (lax.broadcasted_iota(jnp.int32, (heads * ts, tk), 1) % heads
                == lax.broadcasted_iota(jnp.int32, (heads * ts, tk), 0) // ts)
    per_block = tk // heads
    for j in reversed(range(frames // per_block)):
        rows = slice(j * per_block, (j + 1) * per_block)
        kb = ck_ref[rows].reshape(tk, dim).astype(BF16)
        vb = cv_ref[rows].reshape(tk, dim).astype(BF16)
        out, aft = _sb_block(q, kb, vb, uc_ref[...], aft, own_head)
        acc = acc + out
    for h, sl in enumerate(head_cols):
        o_ref[:, sl] = acc[h * ts:(h + 1) * ts].astype(o_ref.dtype)


def _sb_step(q, ck, cv, layer, nk, nv):
    n, ts, w = q.shape
    p = ck.shape[2]
    tk = min(SB_TILE, p)
    new = pl.BlockSpec((None, ts, w), lambda i: (i, 0, 0))
    old = pl.BlockSpec((None, None, p, B_HEADS, B_DIM), lambda i: (layer, i, 0, 0, 0))
    return pl.pallas_call(
        _sb_step_kernel,
        grid=(n,),
        in_specs=[new, old, old, new, new,
                  pl.BlockSpec((ts, ts), lambda i: (0, 0)),
                  pl.BlockSpec((tk, tk), lambda i: (0, 0))],
        out_specs=new,
        out_shape=jax.ShapeDtypeStruct((n, ts, w), BF16),
        compiler_params=_params("parallel"),
        name="sb_step",
    )(q, ck, cv, nk, nv, _suffix_ones(ts), _suffix_ones(tk))


def _merge_kernel(x_ref, tok_ref, qm_ref, mk_ref, mv_ref, w_ref, o_ref, mo_ref):
    head_cols = [slice(h * MEM_DIM, (h + 1) * MEM_DIM) for h in range(N_MEM_HEADS)]
    if len(mk_ref.shape) == 3:
        tm = qm_ref.shape[0]
        m_tok, heads, dim = mk_ref.shape
        q = jnp.concatenate([qm_ref[:, sl] for sl in head_cols], axis=0)
        s = _dot_t(q, mk_ref[...].reshape(m_tok * heads, dim).astype(BF16))
        own_head = (lax.broadcasted_iota(jnp.int32, s.shape, 1) % heads
                    == lax.broadcasted_iota(jnp.int32, s.shape, 0) // tm)
        s = jnp.where(own_head, s, NEG_INF)
        p = jnp.exp(s - jnp.max(s, axis=-1, keepdims=True))
        l = jnp.sum(p, axis=-1, keepdims=True)
        mo = _dot(p.astype(BF16), mv_ref[...].reshape(m_tok * heads, dim).astype(BF16)) / l
        for h, sl in enumerate(head_cols):
            mo_ref[:, sl] = mo[h * tm:(h + 1) * tm].astype(BF16)
    else:
        for sl in head_cols:
            s = _dot_t(qm_ref[:, sl], mk_ref[:, sl])
            p = jnp.exp(s - jnp.max(s, axis=-1, keepdims=True))
            l = jnp.sum(p, axis=-1, keepdims=True)
            mo_ref[:, sl] = (_dot(p.astype(BF16), mv_ref[:, sl]) / l).astype(BF16)
    y = _dot(tok_ref[...], w_ref[:TOK_W, :]) + _dot(mo_ref[...], w_ref[TOK_W:, :])
    o_ref[...] = x_ref[...] + y


def _merge(x, tok, qm, mk, mv, w_out, layer):
    b, t, d = x.shape
    tm = min(TOKEN_TILE, t)
    row = lambda width: pl.BlockSpec((None, tm, width), lambda bi, i: (bi, i, 0))
    if mk.ndim == 5:
        mem = pl.BlockSpec((None, None) + mk.shape[2:], lambda bi, i: (layer, bi, 0, 0, 0))
    else:
        mem = pl.BlockSpec((None,) + mk.shape[1:], lambda bi, i: (bi, 0, 0))
    return pl.pallas_call(
        _merge_kernel,
        grid=(b, t // tm),
        in_specs=[row(d), row(TOK_W), row(MEM_W), mem, mem,
                  pl.BlockSpec((None,) + w_out.shape[1:], lambda bi, i: (layer, 0, 0),
                               pipeline_mode=pl.Buffered(1))],
        out_specs=row(d),
        out_shape=jax.ShapeDtypeStruct((b, t, d), F32),
        scratch_shapes=[pltpu.VMEM((tm, MEM_W), BF16)],
        compiler_params=_params("parallel", "parallel"),
        name="merge",
    )(x, tok, qm, mk, mv, w_out)


def kernel(x_prompt, x_sample, mem_prompt, cache_a_k, cache_a_v, cache_b_k, cache_b_v, cache_mem_k, cache_mem_v, ffn1_norm, ffn1_w_in, ffn1_w_out, attn_norm, w_in, w_out, a_q_gain, a_k_gain, a_rel_bias, mem_norm, w_mem_kv, mem_q_gain, mem_k_gain, ffn2_norm, ffn2_w_in, ffn2_w_out):
    depth = w_in.shape[0]
    bp, tp, d = x_prompt.shape
    bs, ts, _ = x_sample.shape
    n_mem = mem_prompt.shape[1]
    past = cache_b_k.shape[2]
    a_keep = min(BAND_PAST, tp)

    ffn1_w_in, ffn1_w_out, ffn2_w_in, ffn2_w_out, w_in, w_out, w_mem_kv = (
        w.astype(BF16) for w in (ffn1_w_in, ffn1_w_out, ffn2_w_in, ffn2_w_out, w_in, w_out, w_mem_kv))

    x_p = x_prompt.reshape(bp * tp, d)
    x_s = x_sample.reshape(bs * ts, d)
    mem = mem_prompt.reshape(bp * n_mem, d)
    assert a_keep == min(TOKEN_TILE, tp), "band cache rows must be exactly the last token tile of a sequence"
    a_k_p, a_v_p, m_k_p, m_v_p = [], [], [], []
    n_layers = {True: (depth + 1) // 2, False: depth // 2}
    new_kv_p = None
    new_kv_s = {True: None, False: None}
    for l in range(depth):
        j = l // 2
        band = l % 2 == 0
        heads, dim = (A_HEADS, A_DIM) if band else (B_HEADS, B_DIM)
        x_p = _ffn(x_p, ffn1_norm[l], ffn1_w_in, ffn1_w_out, l)
        x_s = _ffn(x_s, ffn1_norm[l], ffn1_w_in, ffn1_w_out, l)

        segments = [(0, TOK_W, dim, 0 if band else None, dim ** -0.5),
                    (TOK_W, TOK_W, dim, 1 if band else None, 1.0),
                    (2 * TOK_W, TOK_W, dim, None, 1.0),
                    (3 * TOK_W, MEM_W, MEM_DIM, 2 if band else 0, MEM_DIM ** -0.5)]
        gains = [a_q_gain[j], a_k_gain[j], mem_q_gain[l]] if band else [mem_q_gain[l]]
        dtypes = [[BF16], [BF16, F32], [BF16, F32], [BF16]]
        if band:
            q_p, k_p, kf_p, v_p, vf_p, qm_p = _proj(x_p, attn_norm[l], w_in, l, segments, gains, dtypes, tail_of=tp)
            a_k_p.append(kf_p.reshape(bp, a_keep, A_HEADS, A_DIM))
            a_v_p.append(vf_p.reshape(bp, a_keep, A_HEADS, A_DIM))
        else:
            q_p, k_p, kf_p, v_p, vf_p, qm_p = _proj(x_p, attn_norm[l], w_in, l, segments, gains, dtypes,
                                                    (j, n_layers[band], new_kv_p))
            new_kv_p = [kf_p, vf_p]
        q_s, k_s, kf_s, v_s, vf_s, qm_s = _proj(x_s, attn_norm[l], w_in, l, segments, gains, dtypes,
                                                (j, n_layers[band], new_kv_s[band]))
        new_kv_s[band] = [kf_s, vf_s]
        in_p = lambda a: a.reshape(bp, tp, -1)
        in_s = lambda a: a.reshape(bs, ts, -1)

        if band:
            tok_p = _band_prompt(in_p(q_p), in_p(k_p), in_p(v_p), a_rel_bias[j])
            tok_s = _band_step(in_s(q_s), cache_a_k, cache_a_v, j, in_s(k_s), in_s(v_s), a_rel_bias[j], past)
        else:
            tok_p = _sb_prompt(in_p(q_p), in_p(k_p), in_p(v_p))
            tok_s = _sb_step(in_s(q_s), cache_b_k, cache_b_v, j, in_s(k_s), in_s(v_s))

        mem_segments = [(0, MEM_W, MEM_DIM, 0, 1.0), (MEM_W, MEM_W, MEM_DIM, None, 1.0)]
        mk, mkf, mv, mvf = _proj(mem, mem_norm[l], w_mem_kv, l, mem_segments, [mem_k_gain[l]],
                                 [[BF16, F32], [BF16, F32]])
        m_k_p.append(mkf.reshape(bp, n_mem, N_MEM_HEADS, MEM_DIM))
        m_v_p.append(mvf.reshape(bp, n_mem, N_MEM_HEADS, MEM_DIM))

        x_p = _merge(in_p(x_p), tok_p, in_p(qm_p), mk.reshape(bp, n_mem, MEM_W), mv.reshape(bp, n_mem, MEM_W),
                     w_out, l).reshape(bp * tp, d)
        x_s = _merge(in_s(x_s), tok_s, in_s(qm_s), cache_mem_k, cache_mem_v, w_out, l).reshape(bs * ts, d)

        x_p = _ffn(x_p, ffn2_norm[l], ffn2_w_in, ffn2_w_out, l)
        x_s = _ffn(x_s, ffn2_norm[l], ffn2_w_in, ffn2_w_out, l)

    b_k_p, b_v_p = (c.reshape(-1, bp, tp, B_HEADS, B_DIM) for c in new_kv_p)
    a_k_s, a_v_s = (c.reshape(-1, bs, ts, A_HEADS, A_DIM) for c in new_kv_s[True])
    b_k_s, b_v_s = (c.reshape(-1, bs, ts, B_HEADS, B_DIM) for c in new_kv_s[False])
    return (x_p.reshape(bp, tp, d), x_s.reshape(bs, ts, d), jnp.stack(a_k_p), jnp.stack(a_v_p), b_k_p, b_v_p,
            jnp.stack(m_k_p), jnp.stack(m_v_p), a_k_s, a_v_s, b_k_s, b_v_s)
```

```python
import functools

import jax
import jax.numpy as jnp
import numpy as np
from jax import lax
from jax.experimental import pallas as pl
from jax.experimental.pallas import tpu as pltpu

CHUNK = 64
LEFT_CHUNKS = 8
BAND_PAST = LEFT_CHUNKS * CHUNK
REL_CLIP = 128
TOK_W = 1024
A_HEADS, A_DIM = 8, 128
B_HEADS, B_DIM = 4, 256
N_MEM_HEADS, MEM_DIM = 4, 128
MEM_W = N_MEM_HEADS * MEM_DIM
EPS = 1e-6
NEG_INF = -1e30
SB_NEG = -1e4

V7X_VMEM_LIMIT_BYTES = 56 * 1024 * 1024
TOKEN_TILE = 512
FF_TILE = 1024
BF16_ROWS = 16
BAND_Q_TILE = BAND_PAST // 2
SB_TILE = 256
SB_EXIT_MASS = 120.0

BF16 = jnp.bfloat16
F32 = jnp.float32


def _params(*semantics):
    return pltpu.CompilerParams(dimension_semantics=semantics, vmem_limit_bytes=V7X_VMEM_LIMIT_BYTES)


def _dot(a, b):
    return jnp.dot(a, b, preferred_element_type=F32)


def _dot_t(a, b):
    return lax.dot_general(a, b, (((1,), (1,)), ((), ())), preferred_element_type=F32)


def _normed_rows(x, g):
    r = lax.rsqrt(jnp.mean(x * x, axis=-1, keepdims=True) + EPS)
    return (x * r * g).astype(BF16)


def _softplus(z):
    return jnp.log(1.0 + jnp.exp(z))


def _ffn_kernel(x_ref, g_ref, wa_ref, wb_ref, wo_ref, o_ref, n_ref, x_keep_ref, *, nf, part_rows):
    i = pl.program_id(0)
    j = pl.program_id(1)
    slot = i % 2

    @pl.when(jnp.logical_and(i == 0, j == 0))
    def _():
        n_ref[0] = _normed_rows(x_ref[...], g_ref[...])
        x_keep_ref[...] = x_ref[...]

    @pl.when(i > 0)
    def _():
        @pl.when(j == 0)
        def _():
            o_ref[...] = x_keep_ref[...]

        n = n_ref[1 - slot]
        a = _dot(n, wa_ref[...])
        b = _dot(n, wb_ref[...])
        h = (a / (1.0 + jnp.exp(-a)) * b * 0.5).astype(BF16)
        o_ref[...] += _dot(h, wo_ref[...])

        start = jnp.minimum(j * part_rows, x_ref.shape[0] - part_rows)
        part = pl.ds(pl.multiple_of(start, BF16_ROWS), part_rows)
        rows = x_ref[part, :]
        n_ref[slot, part, :] = _normed_rows(rows, g_ref[...])
        x_keep_ref[part, :] = rows


def _ffn(x, g, w_in, w_out, layer):
    n, d = x.shape
    f = w_out.shape[1]
    tm = min(TOKEN_TILE, n)
    tf = min(FF_TILE, f)
    nf = f // tf
    n_tiles = n // tm
    part_rows = -(-tm // (nf * BF16_ROWS)) * BF16_ROWS
    hidden = lambda i, j: jnp.where(i == 0, 0, j)
    return pl.pallas_call(
        functools.partial(_ffn_kernel, nf=nf, part_rows=part_rows),
        grid=(n_tiles + 1, nf),
        in_specs=[
            pl.BlockSpec((tm, d), lambda i, j: (jnp.minimum(i, n_tiles - 1), 0)),
            pl.BlockSpec((1, d), lambda i, j: (0, 0)),
            pl.BlockSpec((None, d, tf), lambda i, j: (layer, 0, hidden(i, j))),
            pl.BlockSpec((None, d, tf), lambda i, j: (layer, 0, hidden(i, j) + nf)),
            pl.BlockSpec((None, tf, d), lambda i, j: (layer, hidden(i, j), 0)),
        ],
        out_specs=pl.BlockSpec((tm, d), lambda i, j: (jnp.maximum(i - 1, 0), 0)),
        out_shape=jax.ShapeDtypeStruct((n, d), F32),
        scratch_shapes=[pltpu.VMEM((2, tm, d), BF16), pltpu.VMEM((tm, d), F32)],
        compiler_params=_params("arbitrary", "arbitrary"),
        name="ffn",
    )(x, g.reshape(1, d), w_in, w_in, w_out)


def _proj_kernel(*refs, segments, group, n_gain, n_carried):
    x_ref, g_ref, w_ref = refs[:3]
    gain_refs = refs[3:3 + n_gain]
    out_refs = refs[3 + n_gain + n_carried:]
    n = _normed_rows(x_ref[...], g_ref[...])
    o = 0
    for col0, width, head_dim, gain_idx, scale, n_outs in segments:
        for c in range(0, width, group):
            y = _dot(n, w_ref[:, col0 + c:col0 + c + group])
            for h0 in range(0, group, head_dim):
                yh = y[:, h0:h0 + head_dim]
                if gain_idx is not None:
                    r = lax.rsqrt(jnp.mean(yh * yh, axis=-1, keepdims=True) + EPS)
                    yh = yh * r * gain_refs[gain_idx][...]
                if scale != 1.0:
                    yh = yh * scale
                for t in range(n_outs):
                    ref = out_refs[o + t]
                    if len(ref.shape) == 3:
                        ref[:, (c + h0) // head_dim, :] = yh.astype(ref.dtype)
                    else:
                        ref[:, c + h0:c + h0 + head_dim] = yh.astype(ref.dtype)
        o += n_outs


def _proj(x, g, w, layer, segments, gains, out_dtypes, cache_slot=None, tail_of=None):
    n, d = x.shape
    tm = min(TOKEN_TILE, n)
    group = 512
    segs = tuple((c0, wd, hd, gi, sc, len(out_dtypes[s])) for s, (c0, wd, hd, gi, sc) in enumerate(segments))
    out_shape, out_specs, cache_outs = [], [], []
    for (c0, wd, hd, gi, sc), dts in zip(segments, out_dtypes):
        for dt in dts:
            if tail_of is not None and dt == F32:
                out_shape.append(jax.ShapeDtypeStruct((n // tail_of * tm, wd), dt))
                out_specs.append(pl.BlockSpec((tm, wd), functools.partial(lambda i, s: (i // s, 0), s=tail_of // tm)))
            elif cache_slot is not None and dt == F32:
                slot, n_slots, _ = cache_slot
                cache_outs.append(len(out_shape))
                out_shape.append(jax.ShapeDtypeStruct((n_slots, n, wd // hd, hd), dt))
                out_specs.append(pl.BlockSpec((None, tm, wd // hd, hd),
                                              functools.partial(lambda i, s: (s, i, 0, 0), s=slot)))
            else:
                out_shape.append(jax.ShapeDtypeStruct((n, wd), dt))
                out_specs.append(pl.BlockSpec((tm, wd), lambda i: (i, 0)))
    carried = [] if cache_slot is None or cache_slot[2] is None else list(cache_slot[2])
    n_fixed = 3 + len(gains)
    in_specs = [
        pl.BlockSpec((tm, d), lambda i: (i, 0)),
        pl.BlockSpec((1, d), lambda i: (0, 0)),
        pl.BlockSpec((None,) + w.shape[1:], lambda i: (layer, 0, 0), pipeline_mode=pl.Buffered(1)),
    ] + [pl.BlockSpec((1, gn.shape[-1]), lambda i: (0, 0)) for gn in gains] \
      + [pl.BlockSpec(memory_space=pl.ANY) for _ in carried]
    return pl.pallas_call(
        functools.partial(_proj_kernel, segments=segs, group=group, n_gain=len(gains), n_carried=len(carried)),
        grid=(n // tm,),
        in_specs=in_specs,
        out_specs=out_specs,
        out_shape=out_shape,
        input_output_aliases={n_fixed + c: cache_outs[c] for c in range(len(carried))},
        compiler_params=_params("arbitrary" if tail_of is not None else "parallel"),
        name="proj",
    )(x, g.reshape(1, d), w, *[gn.reshape(1, -1) for gn in gains], *carried)


def _band_kernel(q_ref, k0_ref, k1_ref, k2_ref, v0_ref, v1_ref, v2_ref, bias_ref, o_ref):
    tq = q_ref.shape[0]
    i = pl.program_id(1)
    col = lax.broadcasted_iota(jnp.int32, (tq, 3 * tq), 1)
    in_seq = col >= (2 - i) * tq
    for h in range(A_HEADS):
        sl = slice(h * A_DIM, (h + 1) * A_DIM)
        kh = jnp.concatenate([k0_ref[:, sl], k1_ref[:, sl], k2_ref[:, sl]], axis=0)
        vh = jnp.concatenate([v0_ref[:, sl], v1_ref[:, sl], v2_ref[:, sl]], axis=0)
        s = _dot_t(q_ref[:, sl], kh) + bias_ref[h]
        s = jnp.where(in_seq, s, NEG_INF)
        p = jnp.exp(s - jnp.max(s, axis=-1, keepdims=True))
        l = jnp.sum(p, axis=-1, keepdims=True)
        o_ref[:, sl] = (_dot(p.astype(BF16), vh) / l).astype(o_ref.dtype)


def _band_table(rel_bias, q0, nq, k0, nk):
    heads = rel_bias.shape[0]
    length = nq + nk - 1
    dist = (q0 - k0) + (nq - 1) - np.arange(length)
    n_far = int(np.sum(dist > REL_CLIP))
    n_ahead = int(np.sum(dist < -REL_CLIP))
    parts = [jnp.broadcast_to(rel_bias[:, -1:], (heads, n_far))]
    if n_far + n_ahead < length:
        hi = int(dist[n_far]) + REL_CLIP
        lo = int(dist[length - n_ahead - 1]) + REL_CLIP
        parts.append(rel_bias[:, lo:hi + 1][:, ::-1])
    parts.append(jnp.broadcast_to(rel_bias[:, :1], (heads, n_ahead + 1)))
    e = jnp.concatenate(parts, axis=1).astype(F32)
    period = length + 1
    rows = jnp.tile(e, (1, nq))[:, :nq * (period - 1)].reshape(heads, nq, period - 1)
    table = rows[:, :, nq - 1:nq - 1 + nk]
    q_pos = q0 + np.arange(nq)
    k_pos = k0 + np.arange(nk)
    qc = q_pos // CHUNK
    kc = k_pos // CHUNK
    valid = (k_pos[None, :] >= 0) & (kc[None, :] <= qc[:, None]) & (kc[None, :] >= qc[:, None] - LEFT_CHUNKS)
    return jnp.where(jnp.asarray(valid)[None], table, NEG_INF)


def _band_prompt(q, k, v, rel_bias):
    b, t, w = q.shape
    tq = BAND_Q_TILE
    bias = _band_table(rel_bias, 2 * tq, tq, 0, 3 * tq)
    qspec = pl.BlockSpec((None, tq, w), lambda bi, i: (bi, i, 0))
    kspecs = [pl.BlockSpec((None, tq, w), functools.partial(lambda bi, i, d: (bi, jnp.maximum(i - d, 0), 0), d=d))
              for d in (2, 1, 0)]
    return pl.pallas_call(
        _band_kernel,
        grid=(b, t // tq),
        in_specs=[qspec] + kspecs + kspecs + [
            pl.BlockSpec(bias.shape, lambda bi, i: (0, 0, 0), pipeline_mode=pl.Buffered(1))],
        out_specs=qspec,
        out_shape=jax.ShapeDtypeStruct((b, t, w), BF16),
        compiler_params=_params("parallel", "parallel"),
        name="band_prompt",
    )(q, k, k, k, v, v, v, bias)


def _band_step_kernel(q_ref, ck_ref, cv_ref, nk_ref, nv_ref, bc_ref, bn_ref, o_ref):
    ts = q_ref.shape[0]
    frames, heads, dim = ck_ref.shape
    head_cols = [slice(h * dim, (h + 1) * dim) for h in range(heads)]
    q = jnp.concatenate([q_ref[:, sl] for sl in head_cols], axis=0)
    sc = _dot_t(q, ck_ref[...].reshape(frames * heads, dim).astype(BF16)) + bc_ref[...]
    sn = jnp.concatenate([_dot_t(q_ref[:, sl], nk_ref[:, sl]) + bn_ref[h] for h, sl in enumerate(head_cols)], axis=0)
    m = jnp.maximum(jnp.max(sc, axis=-1, keepdims=True), jnp.max(sn, axis=-1, keepdims=True))
    pc = jnp.exp(sc - m)
    pn = jnp.exp(sn - m)
    l = jnp.sum(pc, axis=-1, keepdims=True) + jnp.sum(pn, axis=-1, keepdims=True)
    o = _dot(pc.astype(BF16), cv_ref[...].reshape(frames * heads, dim).astype(BF16))
    o = o + jnp.concatenate([_dot(pn[h * ts:(h + 1) * ts].astype(BF16), nv_ref[:, sl])
                             for h, sl in enumerate(head_cols)], axis=0)
    o = o / l
    for h, sl in enumerate(head_cols):
        o_ref[:, sl] = o[h * ts:(h + 1) * ts].astype(o_ref.dtype)


def _band_step(q, ck, cv, layer, nk, nv, rel_bias, past):
    n, ts, w = q.shape
    c = ck.shape[2]
    bias_c = _band_table(rel_bias, past, ts, past - c, c)
    bias_n = _band_table(rel_bias, past, ts, past, ts)
    same_head = jnp.eye(A_HEADS, dtype=bool)[:, None, None, :]
    bias_c = jnp.where(same_head, bias_c[:, :, :, None], NEG_INF).reshape(A_HEADS * ts, c * A_HEADS)
    new = pl.BlockSpec((None, ts, w), lambda i: (i, 0, 0))
    old = pl.BlockSpec((None, None, c, A_HEADS, A_DIM), lambda i: (layer, i, 0, 0, 0))
    return pl.pallas_call(
        _band_step_kernel,
        grid=(n,),
        in_specs=[new, old, old, new, new,
                  pl.BlockSpec(bias_c.shape, lambda i: (0, 0)),
                  pl.BlockSpec(bias_n.shape, lambda i: (0, 0, 0))],
        out_specs=new,
        out_shape=jax.ShapeDtypeStruct((n, ts, w), BF16),
        compiler_params=_params("parallel"),
        name="band_step",
    )(q, ck, cv, nk, nv, bias_c, bias_n)


def _suffix_ones(n):
    return (jnp.arange(n)[:, None] >= jnp.arange(n)[None, :]).astype(BF16)


def _sb_block(q, kb, vb, u, aft, mask):
    z = _dot_t(q, kb)
    if mask is not None:
        z = jnp.where(mask, z, SB_NEG)
    loc = _dot(_softplus(z).astype(BF16), u)
    w = jnp.exp(z - loc - aft)
    return _dot(w.astype(BF16), vb), aft + loc[:, 0:1]


def _sb_kernel(q_ref, k_ref, v_ref, u_ref, o_ref, acc_ref, aft_ref):
    t = q_ref.shape[0]
    i = pl.program_id(2)
    q = q_ref[...]
    u = u_ref[...]
    causal = lax.broadcasted_iota(jnp.int32, (t, t), 1) < lax.broadcasted_iota(jnp.int32, (t, t), 0)

    def block(j):
        rows = pl.ds(pl.multiple_of(j * t, t), t)
        return k_ref[rows, :], v_ref[rows, :]

    kd, vd = block(i)
    out_d, aft = _sb_block(q, kd, vd, u, 0.0, causal)
    kp, vp = block(jnp.maximum(i - 1, 0))
    out_p, aft = _sb_block(q, kp, vp, u, aft, i > 0)
    acc_ref[...] = out_d + out_p
    aft_ref[...] = aft

    def more(carry):
        j, least_aft = carry
        return jnp.logical_and(j >= 0, least_aft < SB_EXIT_MASS)

    def older(carry):
        j, _ = carry
        kb, vb = block(j)
        out, aft = _sb_block(q, kb, vb, u, aft_ref[...], None)
        acc_ref[...] += out
        aft_ref[...] = aft
        return j - 1, jnp.min(aft)

    lax.while_loop(more, older, (i - 2, jnp.min(aft)))
    o_ref[...] = acc_ref[...].astype(o_ref.dtype)


def _sb_walk(q, k, v):
    b, t, w = q.shape
    tq = min(SB_TILE, t)
    qspec = pl.BlockSpec((None, tq, B_DIM), lambda bi, h, i: (bi, i, h))
    kspec = pl.BlockSpec((None, t, B_DIM), lambda bi, h, i: (bi, 0, h))
    return pl.pallas_call(
        _sb_kernel,
        grid=(b, B_HEADS, t // tq),
        in_specs=[qspec, kspec, kspec, pl.BlockSpec((tq, tq), lambda bi, h, i: (0, 0))],
        out_specs=qspec,
        out_shape=jax.ShapeDtypeStruct((b, t, w), BF16),
        scratch_shapes=[pltpu.VMEM((tq, B_DIM), F32), pltpu.VMEM((tq, 1), F32)],
        compiler_params=_params("parallel", "parallel", "arbitrary"),
        name="sb_walk",
    )(q, k, v, _suffix_ones(tq))


def _sb_near_kernel(q_ref, kd_ref, vd_ref, kp_ref, vp_ref, u_ref, o_ref, least_ref):
    t = q_ref.shape[0]
    i = pl.program_id(1)
    u = u_ref[...]
    causal = lax.broadcasted_iota(jnp.int32, (t, t), 1) < lax.broadcasted_iota(jnp.int32, (t, t), 0)
    least = None
    for h in range(B_HEADS):
        sl = slice(h * B_DIM, (h + 1) * B_DIM)
        q = q_ref[:, sl]
        out_d, aft = _sb_block(q, kd_ref[:, sl], vd_ref[:, sl], u, 0.0, causal)
        out_p, aft = _sb_block(q, kp_ref[:, sl], vp_ref[:, sl], u, aft, i > 0)
        o_ref[:, sl] = (out_d + out_p).astype(o_ref.dtype)
        row_least = jnp.min(aft, axis=0, keepdims=True)
        least = row_least if least is None else jnp.minimum(least, row_least)
    least_ref[...] = jnp.broadcast_to(jnp.where(i >= 2, least, SB_EXIT_MASS), least_ref.shape)


def _sb_prompt(q, k, v):
    b, t, w = q.shape
    tq = min(SB_TILE, t)
    tile = pl.BlockSpec((None, tq, w), lambda bi, i: (bi, i, 0))
    before = pl.BlockSpec((None, tq, w), lambda bi, i: (bi, jnp.maximum(i - 1, 0), 0))
    near, least = pl.pallas_call(
        _sb_near_kernel,
        grid=(b, t // tq),
        in_specs=[tile, tile, tile, before, before, pl.BlockSpec((tq, tq), lambda bi, i: (0, 0))],
        out_specs=[tile, pl.BlockSpec((None, None, 8, 128), lambda bi, i: (bi, i, 0, 0))],
        out_shape=[jax.ShapeDtypeStruct((b, t, w), BF16), jax.ShapeDtypeStruct((b, t // tq, 8, 128), F32)],
        compiler_params=_params("parallel", "parallel"),
        name="sb_near",
    )(q, k, v, k, v, _suffix_ones(tq))
    return lax.cond(jnp.min(least) < SB_EXIT_MASS, lambda: _sb_walk(q, k, v), lambda: near)


def _sb_step_kernel(q_ref, ck_ref, cv_ref, nk_ref, nv_ref, un_ref, uc_ref, o_ref):
    ts = q_ref.shape[0]
    tk = uc_ref.shape[0]
    frames, heads, dim = ck_ref.shape
    causal = lax.broadcasted_iota(jnp.int32, (ts, ts), 1) < lax.broadcasted_iota(jnp.int32, (ts, ts), 0)
    head_cols = [slice(h * dim, (h + 1) * dim) for h in range(heads)]
    own = [_sb_block(q_ref[:, sl], nk_ref[:, sl], nv_ref[:, sl], un_ref[...], 0.0, causal) for sl in head_cols]
    q = jnp.concatenate([q_ref[:, sl] for sl in head_cols], axis=0)
    acc = jnp.concatenate([o for o, _ in own], axis=0)
    aft = jnp.concatenate([a for _, a in own], axis=0)
    own_head = (lax.broadcasted_iota(jnp.int32, (heads * ts, tk), 1) % heads
                == lax.broadcasted_iota(jnp.int32, (heads * ts, tk), 0) // ts)
    per_block = tk // heads
    for j in reversed(range(frames // per_block)):
        rows = slice(j * per_block, (j + 1) * per_block)
        kb = ck_ref[rows].reshape(tk, dim).astype(BF16)
        vb = cv_ref[rows].reshape(tk, dim).astype(BF16)
        out, aft = _sb_block(q, kb, vb, uc_ref[...], aft, own_head)
        acc = acc + out
    for h, sl in enumerate(head_cols):
        o_ref[:, sl] = acc[h * ts:(h + 1) * ts].astype(o_ref.dtype)


def _sb_step(q, ck, cv, layer, nk, nv):
    n, ts, w = q.shape
    p = ck.shape[2]
    tk = min(SB_TILE, p)
    new = pl.BlockSpec((None, ts, w), lambda i: (i, 0, 0))
    old = pl.BlockSpec((None, None, p, B_HEADS, B_DIM), lambda i: (layer, i, 0, 0, 0))
    return pl.pallas_call(
        _sb_step_kernel,
        grid=(n,),
        in_specs=[new, old, old, new, new,
                  pl.BlockSpec((ts, ts), lambda i: (0, 0)),
                  pl.BlockSpec((tk, tk), lambda i: (0, 0))],
        out_specs=new,
        out_shape=jax.ShapeDtypeStruct((n, ts, w), BF16),
        compiler_params=_params("parallel"),
        name="sb_step",
    )(q, ck, cv, nk, nv, _suffix_ones(ts), _suffix_ones(tk))


def _merge_kernel(x_ref, tok_ref, qm_ref, mk_ref, mv_ref, w_ref, o_ref, mo_ref):
    head_cols = [slice(h * MEM_DIM, (h + 1) * MEM_DIM) for h in range(N_MEM_HEADS)]
    if len(mk_ref.shape) == 3:
        tm = qm_ref.shape[0]
        m_tok, heads, dim = mk_ref.shape
        q = jnp.concatenate([qm_ref[:, sl] for sl in head_cols], axis=0)
        s = _dot_t(q, mk_ref[...].reshape(m_tok * heads, dim).astype(BF16))
        own_head = (lax.broadcasted_iota(jnp.int32, s.shape, 1) % heads
                    == lax.broadcasted_iota(jnp.int32, s.shape, 0) // tm)
        s = jnp.where(own_head, s, NEG_INF)
        p = jnp.exp(s - jnp.max(s, axis=-1, keepdims=True))
        l = jnp.sum(p, axis=-1, keepdims=True)
        mo = _dot(p.astype(BF16), mv_ref[...].reshape(m_tok * heads, dim).astype(BF16)) / l
        for h, sl in enumerate(head_cols):
            mo_ref[:, sl] = mo[h * tm:(h + 1) * tm].astype(BF16)
    else:
        for sl in head_cols:
            s = _dot_t(qm_ref[:, sl], mk_ref[:, sl])
            p = jnp.exp(s - jnp.max(s, axis=-1, keepdims=True))
            l = jnp.sum(p, axis=-1, keepdims=True)
            mo_ref[:, sl] = (_dot(p.astype(BF16), mv_ref[:, sl]) / l).astype(BF16)
    y = _dot(tok_ref[...], w_ref[:TOK_W, :]) + _dot(mo_ref[...], w_ref[TOK_W:, :])
    o_ref[...] = x_ref[...] + y


def _merge(x, tok, qm, mk, mv, w_out, layer):
    b, t, d = x.shape
    tm = min(TOKEN_TILE, t)
    row = lambda width: pl.BlockSpec((None, tm, width), lambda bi, i: (bi, i, 0))
    if mk.ndim == 5:
        mem = pl.BlockSpec((None, None) + mk.shape[2:], lambda bi, i: (layer, bi, 0, 0, 0))
    else:
        mem = pl.BlockSpec((None,) + mk.shape[1:], lambda bi, i: (bi, 0, 0))
    return pl.pallas_call(
        _merge_kernel,
        grid=(b, t // tm),
        in_specs=[row(d), row(TOK_W), row(MEM_W), mem, mem,
                  pl.BlockSpec((None,) + w_out.shape[1:], lambda bi, i: (layer, 0, 0),
                               pipeline_mode=pl.Buffered(1))],
        out_specs=row(d),
        out_shape=jax.ShapeDtypeStruct((b, t, d), F32),
        scratch_shapes=[pltpu.VMEM((tm, MEM_W), BF16)],
        compiler_params=_params("parallel", "parallel"),
        name="merge",
    )(x, tok, qm, mk, mv, w_out)


def kernel(x_prompt, x_sample, mem_prompt, cache_a_k, cache_a_v, cache_b_k, cache_b_v, cache_mem_k, cache_mem_v, ffn1_norm, ffn1_w_in, ffn1_w_out, attn_norm, w_in, w_out, a_q_gain, a_k_gain, a_rel_bias, mem_norm, w_mem_kv, mem_q_gain, mem_k_gain, ffn2_norm, ffn2_w_in, ffn2_w_out):
    depth = w_in.shape[0]
    bp, tp, d = x_prompt.shape
    bs, ts, _ = x_sample.shape
    n_mem = mem_prompt.shape[1]
    past = cache_b_k.shape[2]
    a_keep = min(BAND_PAST, tp)

    ffn1_w_in, ffn1_w_out, ffn2_w_in, ffn2_w_out, w_in, w_out, w_mem_kv = (
        w.astype(BF16) for w in (ffn1_w_in, ffn1_w_out, ffn2_w_in, ffn2_w_out, w_in, w_out, w_mem_kv))

    x_p = x_prompt.reshape(bp * tp, d)
    x_s = x_sample.reshape(bs * ts, d)
    mem = mem_prompt.reshape(bp * n_mem, d)
    assert a_keep == min(TOKEN_TILE, tp), "band cache rows must be exactly the last token tile of a sequence"
    a_k_p, a_v_p, m_k_p, m_v_p = [], [], [], []
    n_layers = {True: (depth + 1) // 2, False: depth // 2}
    new_kv_p = None
    new_kv_s = {True: None, False: None}
    for l in range(depth):
        j = l // 2
        band = l % 2 == 0
        heads, dim = (A_HEADS, A_DIM) if band else (B_HEADS, B_DIM)
        x_p = _ffn(x_p, ffn1_norm[l], ffn1_w_in, ffn1_w_out, l)
        x_s = _ffn(x_s, ffn1_norm[l], ffn1_w_in, ffn1_w_out, l)

        segments = [(0, TOK_W, dim, 0 if band else None, dim ** -0.5),
                    (TOK_W, TOK_W, dim, 1 if band else None, 1.0),
                    (2 * TOK_W, TOK_W, dim, None, 1.0),
                    (3 * TOK_W, MEM_W, MEM_DIM, 2 if band else 0, MEM_DIM ** -0.5)]
        gains = [a_q_gain[j], a_k_gain[j], mem_q_gain[l]] if band else [mem_q_gain[l]]
        dtypes = [[BF16], [BF16, F32], [BF16, F32], [BF16]]
        if band:
            q_p, k_p, kf_p, v_p, vf_p, qm_p = _proj(x_p, attn_norm[l], w_in, l, segments, gains, dtypes, tail_of=tp)
            a_k_p.append(kf_p.reshape(bp, a_keep, A_HEADS, A_DIM))
            a_v_p.append(vf_p.reshape(bp, a_keep, A_HEADS, A_DIM))
        else:
            q_p, k_p, kf_p, v_p, vf_p, qm_p = _proj(x_p, attn_norm[l], w_in, l, segments, gains, dtypes,
                                                    (j, n_layers[band], new_kv_p))
            new_kv_p = [kf_p, vf_p]
        q_s, k_s, kf_s, v_s, vf_s, qm_s = _proj(x_s, attn_norm[l], w_in, l, segments, gains, dtypes,
                                                (j, n_layers[band], new_kv_s[band]))
        new_kv_s[band] = [kf_s, vf_s]
        in_p = lambda a: a.reshape(bp, tp, -1)
        in_s = lambda a: a.reshape(bs, ts, -1)

        if band:
            tok_p = _band_prompt(in_p(q_p), in_p(k_p), in_p(v_p), a_rel_bias[j])
            tok_s = _band_step(in_s(q_s), cache_a_k, cache_a_v, j, in_s(k_s), in_s(v_s), a_rel_bias[j], past)
        else:
            tok_p = _sb_prompt(in_p(q_p), in_p(k_p), in_p(v_p))
            tok_s = _sb_step(in_s(q_s), cache_b_k, cache_b_v, j, in_s(k_s), in_s(v_s))

        mem_segments = [(0, MEM_W, MEM_DIM, 0, 1.0), (MEM_W, MEM_W, MEM_DIM, None, 1.0)]
        mk, mkf, mv, mvf = _proj(mem, mem_norm[l], w_mem_kv, l, mem_segments, [mem_k_gain[l]],
                                 [[BF16, F32], [BF16, F32]])
        m_k_p.append(mkf.reshape(bp, n_mem, N_MEM_HEADS, MEM_DIM))
        m_v_p.append(mvf.reshape(bp, n_mem, N_MEM_HEADS, MEM_DIM))

        x_p = _merge(in_p(x_p), tok_p, in_p(qm_p), mk.reshape(bp, n_mem, MEM_W), mv.reshape(bp, n_mem, MEM_W),
                     w_out, l).reshape(bp * tp, d)
        x_s = _merge(in_s(x_s), tok_s, in_s(qm_s), cache_mem_k, cache_mem_v, w_out, l).reshape(bs * ts, d)

        x_p = _ffn(x_p, ffn2_norm[l], ffn2_w_in, ffn2_w_out, l)
        x_s = _ffn(x_s, ffn2_norm[l], ffn2_w_in, ffn2_w_out, l)

    b_k_p, b_v_p = (c.reshape(-1, bp, tp, B_HEADS, B_DIM) for c in new_kv_p)
    a_k_s, a_v_s = (c.reshape(-1, bs, ts, A_HEADS, A_DIM) for c in new_kv_s[True])
    b_k_s, b_v_s = (c.reshape(-1, bs, ts, B_HEADS, B_DIM) for c in new_kv_s[False])
    return (x_p.reshape(bp, tp, d), x_s.reshape(bs, ts, d), jnp.stack(a_k_p), jnp.stack(a_v_p), b_k_p, b_v_p,
            jnp.stack(m_k_p), jnp.stack(m_v_p), a_k_s, a_v_s, b_k_s, b_v_s)
```

```python
import functools

import jax
import jax.numpy as jnp
import numpy as np
from jax import lax
from jax.experimental import pallas as pl
from jax.experimental.pallas import tpu as pltpu

CHUNK = 64
LEFT_CHUNKS = 8
BAND_PAST = LEFT_CHUNKS * CHUNK
REL_CLIP = 128
TOK_W = 1024
A_HEADS, A_DIM = 8, 128
B_HEADS, B_DIM = 4, 256
N_MEM_HEADS, MEM_DIM = 4, 128
MEM_W = N_MEM_HEADS * MEM_DIM
EPS = 1e-6
NEG_INF = -1e30
SB_NEG = -1e4
LOG2_E = 1.4426950408889634

V7X_VMEM_LIMIT_BYTES = 56 * 1024 * 1024
TOKEN_TILE = 512
FFN_ROW_TILE = 256
FF_TILE = 1024
BAND_Q_TILE = BAND_PAST // 2
SB_TILE = 256
SB_EXIT_MASS = 120.0

BF16 = jnp.bfloat16
F32 = jnp.float32


def _params(*semantics):
    return pltpu.CompilerParams(dimension_semantics=semantics, vmem_limit_bytes=V7X_VMEM_LIMIT_BYTES)


def _dot(a, b):
    return jnp.dot(a, b, preferred_element_type=F32)


def _dot_t(a, b):
    return lax.dot_general(a, b, (((1,), (1,)), ((), ())), preferred_element_type=F32)


def _normed_rows(x, g):
    r = lax.rsqrt(jnp.mean(x * x, axis=-1, keepdims=True) + EPS)
    return (x * r * g).astype(BF16)


def _softplus(z):
    return jnp.log(1.0 + jnp.exp(z))


def _ffn_kernel(x_ref, g_ref, wi_ref, wo_ref, o_ref, n_ref, x_keep_ref, *, chunk):
    i = pl.program_id(0)
    slot = i % 2
    f = wo_ref.shape[0]

    @pl.when(i == 0)
    def _():
        n_ref[0] = _normed_rows(x_ref[...], g_ref[...])
        x_keep_ref[...] = x_ref[...]

    @pl.when(i > 0)
    def _():
        n = n_ref[1 - slot]
        o_ref[...] = x_keep_ref[...]
        for c in range(0, f, chunk):
            a = _dot(n, wi_ref[:, c:c + chunk])
            b = _dot(n, wi_ref[:, f + c:f + c + chunk])
            h = (a / (1.0 + jnp.exp(-a)) * b * 0.5).astype(BF16)
            o_ref[...] += _dot(h, wo_ref[c:c + chunk, :])
        rows = x_ref[...]
        n_ref[slot] = _normed_rows(rows, g_ref[...])
        x_keep_ref[...] = rows


def _ffn(x, g, w_in, w_out, layer):
    n, d = x.shape
    f = w_out.shape[1]
    tm = min(FFN_ROW_TILE, n)
    n_tiles = n // tm
    resident = lambda shape: pl.BlockSpec((None,) + shape, lambda i: (layer, 0, 0), pipeline_mode=pl.Buffered(1))
    return pl.pallas_call(
        functools.partial(_ffn_kernel, chunk=min(FF_TILE, f)),
        grid=(n_tiles + 1,),
        in_specs=[
            pl.BlockSpec((tm, d), lambda i: (jnp.minimum(i, n_tiles - 1), 0)),
            pl.BlockSpec((1, d), lambda i: (0, 0)),
            resident(w_in.shape[1:]),
            resident(w_out.shape[1:]),
        ],
        out_specs=pl.BlockSpec((tm, d), lambda i: (jnp.maximum(i - 1, 0), 0)),
        out_shape=jax.ShapeDtypeStruct((n, d), F32),
        scratch_shapes=[pltpu.VMEM((2, tm, d), BF16), pltpu.VMEM((tm, d), F32)],
        compiler_params=_params("arbitrary"),
        name="ffn",
    )(x, g.reshape(1, d), w_in, w_out)


def _proj_kernel(*refs, segments, group, n_gain, n_carried):
    x_ref, g_ref, w_ref = refs[:3]
    gain_refs = refs[3:3 + n_gain]
    out_refs = refs[3 + n_gain + n_carried:]
    n = _normed_rows(x_ref[...], g_ref[...])
    o = 0
    for col0, width, head_dim, gain_idx, scale, n_outs in segments:
        for c in range(0, width, group):
            y = _dot(n, w_ref[:, col0 + c:col0 + c + group])
            for h0 in range(0, group, head_dim):
                yh = y[:, h0:h0 + head_dim]
                if gain_idx is not None:
                    r = lax.rsqrt(jnp.mean(yh * yh, axis=-1, keepdims=True) + EPS)
                    yh = yh * r * gain_refs[gain_idx][...]
                if scale != 1.0:
                    yh = yh * scale
                for t in range(n_outs):
                    ref = out_refs[o + t]
                    if len(ref.shape) == 3:
                        ref[:, (c + h0) // head_dim, :] = yh.astype(ref.dtype)
                    else:
                        ref[:, c + h0:c + h0 + head_dim] = yh.astype(ref.dtype)
        o += n_outs


def _proj(x, g, w, layer, segments, gains, out_dtypes, cache_slot=None, tail_of=None):
    n, d = x.shape
    tm = min(TOKEN_TILE, n)
    group = 512
    segs = tuple((c0, wd, hd, gi, sc, len(out_dtypes[s])) for s, (c0, wd, hd, gi, sc) in enumerate(segments))
    out_shape, out_specs, cache_outs = [], [], []
    for (c0, wd, hd, gi, sc), dts in zip(segments, out_dtypes):
        for dt in dts:
            if tail_of is not None and dt == F32:
                out_shape.append(jax.ShapeDtypeStruct((n // tail_of * tm, wd), dt))
                out_specs.append(pl.BlockSpec((tm, wd), functools.partial(lambda i, s: (i // s, 0), s=tail_of // tm)))
            elif cache_slot is not None and dt == F32:
                slot, carried = cache_slot
                assert carried[len(cache_outs)].shape[1:] == (n, wd // hd, hd)
                out_shape.append(jax.ShapeDtypeStruct(carried[len(cache_outs)].shape, dt))
                cache_outs.append(len(out_shape) - 1)
                out_specs.append(pl.BlockSpec((None, tm, wd // hd, hd),
                                              functools.partial(lambda i, s: (s, i, 0, 0), s=slot)))
            else:
                out_shape.append(jax.ShapeDtypeStruct((n, wd), dt))
                out_specs.append(pl.BlockSpec((tm, wd), lambda i: (i, 0)))
    carried = [] if cache_slot is None else list(cache_slot[1])
    n_fixed = 3 + len(gains)
    in_specs = [
        pl.BlockSpec((tm, d), lambda i: (i, 0)),
        pl.BlockSpec((1, d), lambda i: (0, 0)),
        pl.BlockSpec((None,) + w.shape[1:], lambda i: (layer, 0, 0), pipeline_mode=pl.Buffered(1)),
    ] + [pl.BlockSpec((1, gn.shape[-1]), lambda i: (0, 0)) for gn in gains] \
      + [pl.BlockSpec(memory_space=pl.ANY) for _ in carried]
    return pl.pallas_call(
        functools.partial(_proj_kernel, segments=segs, group=group, n_gain=len(gains), n_carried=len(carried)),
        grid=(n // tm,),
        in_specs=in_specs,
        out_specs=out_specs,
        out_shape=out_shape,
        input_output_aliases={n_fixed + c: cache_outs[c] for c in range(len(carried))},
        compiler_params=_params("arbitrary" if tail_of is not None else "parallel"),
        name="proj",
    )(x, g.reshape(1, d), w, *[gn.reshape(1, -1) for gn in gains], *carried)


def _band_kernel(q_ref, k0_ref, k1_ref, k2_ref, v0_ref, v1_ref, v2_ref, bias_ref, o_ref):
    for h in range(A_HEADS):
        sl = slice(h * A_DIM, (h + 1) * A_DIM)
        kh = jnp.concatenate([k0_ref[:, sl], k1_ref[:, sl], k2_ref[:, sl]], axis=0)
        vh = jnp.concatenate([v0_ref[:, sl], v1_ref[:, sl], v2_ref[:, sl]], axis=0)
        s = _dot_t(q_ref[:, sl], kh) + bias_ref[h]
        p = jnp.exp2(s - jnp.max(s, axis=-1, keepdims=True))
        l = jnp.sum(p, axis=-1, keepdims=True)
        o_ref[:, sl] = (_dot(p.astype(BF16), vh) / l).astype(o_ref.dtype)


def _band_table(rel_bias, q0, nq, k0, nk):
    heads = rel_bias.shape[0]
    length = nq + nk - 1
    dist = (q0 - k0) + (nq - 1) - np.arange(length)
    n_far = int(np.sum(dist > REL_CLIP))
    n_ahead = int(np.sum(dist < -REL_CLIP))
    parts = [jnp.broadcast_to(rel_bias[:, -1:], (heads, n_far))]
    if n_far + n_ahead < length:
        hi = int(dist[n_far]) + REL_CLIP
        lo = int(dist[length - n_ahead - 1]) + REL_CLIP
        parts.append(rel_bias[:, lo:hi + 1][:, ::-1])
    parts.append(jnp.broadcast_to(rel_bias[:, :1], (heads, n_ahead + 1)))
    e = jnp.concatenate(parts, axis=1).astype(F32)
    period = length + 1
    rows = jnp.tile(e, (1, nq))[:, :nq * (period - 1)].reshape(heads, nq, period - 1)
    table = rows[:, :, nq - 1:nq - 1 + nk]
    q_pos = q0 + np.arange(nq)
    k_pos = k0 + np.arange(nk)
    qc = q_pos // CHUNK
    kc = k_pos // CHUNK
    valid = (k_pos[None, :] >= 0) & (kc[None, :] <= qc[:, None]) & (kc[None, :] >= qc[:, None] - LEFT_CHUNKS)
    return jnp.where(jnp.asarray(valid)[None], table * LOG2_E, NEG_INF)


def _band_prompt(q, k, v, rel_bias):
    b, t, w = q.shape
    tq = BAND_Q_TILE
    bias = jnp.stack([_band_table(rel_bias, i * tq, tq, (i - 2) * tq, 3 * tq) for i in range(3)])
    qspec = pl.BlockSpec((None, tq, w), lambda bi, i: (bi, i, 0))
    kspecs = [pl.BlockSpec((None, tq, w), functools.partial(lambda bi, i, d: (bi, jnp.maximum(i - d, 0), 0), d=d))
              for d in (2, 1, 0)]
    return pl.pallas_call(
        _band_kernel,
        grid=(b, t // tq),
        in_specs=[qspec] + kspecs + kspecs + [
            pl.BlockSpec((None,) + bias.shape[1:], lambda bi, i: (jnp.minimum(i, 2), 0, 0, 0))],
        out_specs=qspec,
        out_shape=jax.ShapeDtypeStruct((b, t, w), BF16),
        compiler_params=_params("parallel", "parallel"),
        name="band_prompt",
    )(q, k, k, k, v, v, v, bias)


def _band_step_kernel(q_ref, ck_ref, cv_ref, nk_ref, nv_ref, bc_ref, bn_ref, o_ref):
    ts = q_ref.shape[0]
    frames, heads, dim = ck_ref.shape
    head_cols = [slice(h * dim, (h + 1) * dim) for h in range(heads)]
    q = jnp.concatenate([q_ref[:, sl] for sl in head_cols], axis=0)
    sc = _dot_t(q, ck_ref[...].reshape(frames * heads, dim).astype(BF16)) + bc_ref[...]
    sn = jnp.concatenate([_dot_t(q_ref[:, sl], nk_ref[:, sl]) + bn_ref[h] for h, sl in enumerate(head_cols)], axis=0)
    m = jnp.maximum(jnp.max(sc, axis=-1, keepdims=True), jnp.max(sn, axis=-1, keepdims=True))
    pc = jnp.exp2(sc - m)
    pn = jnp.exp2(sn - m)
    l = jnp.sum(pc, axis=-1, keepdims=True) + jnp.sum(pn, axis=-1, keepdims=True)
    o = _dot(pc.astype(BF16), cv_ref[...].reshape(frames * heads, dim).astype(BF16))
    o = o + jnp.concatenate([_dot(pn[h * ts:(h + 1) * ts].astype(BF16), nv_ref[:, sl])
                             for h, sl in enumerate(head_cols)], axis=0)
    o = o / l
    for h, sl in enumerate(head_cols):
        o_ref[:, sl] = o[h * ts:(h + 1) * ts].astype(o_ref.dtype)


def _band_step(q, ck, cv, layer, nk, nv, rel_bias, past):
    n, ts, w = q.shape
    c = ck.shape[2]
    bias_c = _band_table(rel_bias, past, ts, past - c, c)
    bias_n = _band_table(rel_bias, past, ts, past, ts)
    same_head = jnp.eye(A_HEADS, dtype=bool)[:, None, None, :]
    bias_c = jnp.where(same_head, bias_c[:, :, :, None], NEG_INF).reshape(A_HEADS * ts, c * A_HEADS)
    new = pl.BlockSpec((None, ts, w), lambda i: (i, 0, 0))
    old = pl.BlockSpec((None, None, c, A_HEADS, A_DIM), lambda i: (layer, i, 0, 0, 0))
    return pl.pallas_call(
        _band_step_kernel,
        grid=(n,),
        in_specs=[new, old, old, new, new,
                  pl.BlockSpec(bias_c.shape, lambda i: (0, 0)),
                  pl.BlockSpec(bias_n.shape, lambda i: (0, 0, 0))],
        out_specs=new,
        out_shape=jax.ShapeDtypeStruct((n, ts, w), BF16),
        compiler_params=_params("parallel"),
        name="band_step",
    )(q, ck, cv, nk, nv, bias_c, bias_n)


def _suffix_ones(n):
    return (jnp.arange(n)[:, None] >= jnp.arange(n)[None, :]).astype(BF16)


def _sb_block(q, kb, vb, u, aft, mask):
    z = _dot_t(q, kb)
    if mask is not None:
        z = jnp.where(mask, z, SB_NEG)
    loc = _dot(_softplus(z).astype(BF16), u)
    w = jnp.exp(z - loc - aft)
    return _dot(w.astype(BF16), vb), aft + loc[:, 0:1]


def _sb_kernel(q_ref, k_ref, v_ref, u_ref, o_ref, acc_ref, aft_ref):
    t = q_ref.shape[0]
    i = pl.program_id(2)
    q = q_ref[...]
    u = u_ref[...]
    causal = lax.broadcasted_iota(jnp.int32, (t, t), 1) < lax.broadcasted_iota(jnp.int32, (t, t), 0)

    def block(j):
        rows = pl.ds(pl.multiple_of(j * t, t), t)
        return k_ref[rows, :], v_ref[rows, :]

    kd, vd = block(i)
    out_d, aft = _sb_block(q, kd, vd, u, 0.0, causal)
    kp, vp = block(jnp.maximum(i - 1, 0))
    out_p, aft = _sb_block(q, kp, vp, u, aft, i > 0)
    acc_ref[...] = out_d + out_p
    aft_ref[...] = aft

    def more(carry):
        j, least_aft = carry
        return jnp.logical_and(j >= 0, least_aft < SB_EXIT_MASS)

    def older(carry):
        j, _ = carry
        kb, vb = block(j)
        out, aft = _sb_block(q, kb, vb, u, aft_ref[...], None)
        acc_ref[...] += out
        aft_ref[...] = aft
        return j - 1, jnp.min(aft)

    lax.while_loop(more, older, (i - 2, jnp.min(aft)))
    o_ref[...] = acc_ref[...].astype(o_ref.dtype)


def _sb_walk(q, k, v):
    b, t, w = q.shape
    tq = min(SB_TILE, t)
    qspec = pl.BlockSpec((None, tq, B_DIM), lambda bi, h, i: (bi, i, h))
    kspec = pl.BlockSpec((None, t, B_DIM), lambda bi, h, i: (bi, 0, h))
    return pl.pallas_call(
        _sb_kernel,
        grid=(b, B_HEADS, t // tq),
        in_specs=[qspec, kspec, kspec, pl.BlockSpec((tq, tq), lambda bi, h, i: (0, 0))],
        out_specs=qspec,
        out_shape=jax.ShapeDtypeStruct((b, t, w), BF16),
        scratch_shapes=[pltpu.VMEM((tq, B_DIM), F32), pltpu.VMEM((tq, 1), F32)],
        compiler_params=_params("parallel", "parallel", "arbitrary"),
        name="sb_walk",
    )(q, k, v, _suffix_ones(tq))


def _sb_near_kernel(q_ref, kd_ref, vd_ref, kp_ref, vp_ref, u_ref, o_ref, least_ref):
    t = q_ref.shape[0]
    i = pl.program_id(1)
    u = u_ref[...]
    causal = lax.broadcasted_iota(jnp.int32, (t, t), 1) < lax.broadcasted_iota(jnp.int32, (t, t), 0)
    least = None
    for h in range(B_HEADS):
        sl = slice(h * B_DIM, (h + 1) * B_DIM)
        q = q_ref[:, sl]
        out_d, aft = _sb_block(q, kd_ref[:, sl], vd_ref[:, sl], u, 0.0, causal)
        out_p, aft = _sb_block(q, kp_ref[:, sl], vp_ref[:, sl], u, aft, i > 0)
        o_ref[:, sl] = (out_d + out_p).astype(o_ref.dtype)
        row_least = jnp.min(aft, axis=0, keepdims=True)
        least = row_least if least is None else jnp.minimum(least, row_least)
    least_ref[...] = jnp.broadcast_to(jnp.where(i >= 2, least, SB_EXIT_MASS), least_ref.shape)


def _sb_prompt(q, k, v):
    b, t, w = q.shape
    tq = min(SB_TILE, t)
    tile = pl.BlockSpec((None, tq, w), lambda bi, i: (bi, i, 0))
    before = pl.BlockSpec((None, tq, w), lambda bi, i: (bi, jnp.maximum(i - 1, 0), 0))
    near, least = pl.pallas_call(
        _sb_near_kernel,
        grid=(b, t // tq),
        in_specs=[tile, tile, tile, before, before, pl.BlockSpec((tq, tq), lambda bi, i: (0, 0))],
        out_specs=[tile, pl.BlockSpec((None, None, 8, 128), lambda bi, i: (bi, i, 0, 0))],
        out_shape=[jax.ShapeDtypeStruct((b, t, w), BF16), jax.ShapeDtypeStruct((b, t // tq, 8, 128), F32)],
        compiler_params=_params("parallel", "parallel"),
        name="sb_near",
    )(q, k, v, k, v, _suffix_ones(tq))
    return lax.cond(jnp.min(least) < SB_EXIT_MASS, lambda: _sb_walk(q, k, v), lambda: near)


def _sb_step_kernel(q_ref, ck_ref, cv_ref, nk_ref, nv_ref, un_ref, uc_ref, o_ref):
    ts = q_ref.shape[0]
    tk = uc_ref.shape[0]
    frames, heads, dim = ck_ref.shape
    causal = lax.broadcasted_iota(jnp.int32, (ts, ts), 1) < lax.broadcasted_iota(jnp.int32, (ts, ts), 0)
    head_cols = [slice(h * dim, (h + 1) * dim) for h in range(heads)]
    own = [_sb_block(q_ref[:, sl], nk_ref[:, sl], nv_ref[:, sl], un_ref[...], 0.0, causal) for sl in head_cols]
    q = jnp.concatenate([q_ref[:, sl] for sl in head_cols], axis=0)
    acc = jnp.concatenate([o for o, _ in own], axis=0)
    aft = jnp.concatenate([a for _, a in own], axis=0)
    own_head = (lax.broadcasted_iota(jnp.int32, (heads * ts, tk), 1) % heads
                == lax.broadcasted_iota(jnp.int32, (heads * ts, tk), 0) // ts)
    per_block = tk // heads
    for j in reversed(range(frames // per_block)):
        rows = slice(j * per_block, (j + 1) * per_block)
        kb = ck_ref[rows].reshape(tk, dim).astype(BF16)
        vb = cv_ref[rows].reshape(tk, dim).astype(BF16)
        out, aft = _sb_block(q, kb, vb, uc_ref[...], aft, own_head)
        acc = acc + out
    for h, sl in enumerate(head_cols):
        o_ref[:, sl] = acc[h * ts:(h + 1) * ts].astype(o_ref.dtype)


def _sb_step(q, ck, cv, layer, nk, nv):
    n, ts, w = q.shape
    p = ck.shape[2]
    tk = min(SB_TILE, p)
    new = pl.BlockSpec((None, ts, w), lambda i: (i, 0, 0))
    old = pl.BlockSpec((None, None, p, B_HEADS, B_DIM), lambda i: (layer, i, 0, 0, 0))
    return pl.pallas_call(
        _sb_step_kernel,
        grid=(n,),
        in_specs=[new, old, old, new, new,
                  pl.BlockSpec((ts, ts), lambda i: (0, 0)),
                  pl.BlockSpec((tk, tk), lambda i: (0, 0))],
        out_specs=new,
        out_shape=jax.ShapeDtypeStruct((n, ts, w), BF16),
        compiler_params=_params("parallel"),
        name="sb_step",
    )(q, ck, cv, nk, nv, _suffix_ones(ts), _suffix_ones(tk))


def _merge_kernel(x_ref, tok_ref, qm_ref, mk_ref, mv_ref, w_ref, o_ref, mo_ref):
    head_cols = [slice(h * MEM_DIM, (h + 1) * MEM_DIM) for h in range(N_MEM_HEADS)]
    if len(mk_ref.shape) == 3:
        tm = qm_ref.shape[0]
        m_tok, heads, dim = mk_ref.shape
        q = jnp.concatenate([qm_ref[:, sl] for sl in head_cols], axis=0)
        s = _dot_t(q, mk_ref[...].reshape(m_tok * heads, dim).astype(BF16))
        own_head = (lax.broadcasted_iota(jnp.int32, s.shape, 1) % heads
                    == lax.broadcasted_iota(jnp.int32, s.shape, 0) // tm)
        s = jnp.where(own_head, s, NEG_INF)
        p = jnp.exp(s - jnp.max(s, axis=-1, keepdims=True))
        l = jnp.sum(p, axis=-1, keepdims=True)
        mo = _dot(p.astype(BF16), mv_ref[...].reshape(m_tok * heads, dim).astype(BF16)) / l
        for h, sl in enumerate(head_cols):
            mo_ref[:, sl] = mo[h * tm:(h + 1) * tm].astype(BF16)
    else:
        for sl in head_cols:
            s = _dot_t(qm_ref[:, sl], mk_ref[:, sl])
            p = jnp.exp(s - jnp.max(s, axis=-1, keepdims=True))
            l = jnp.sum(p, axis=-1, keepdims=True)
            mo_ref[:, sl] = (_dot(p.astype(BF16), mv_ref[:, sl]) / l).astype(BF16)
    y = _dot(tok_ref[...], w_ref[:TOK_W, :]) + _dot(mo_ref[...], w_ref[TOK_W:, :])
    o_ref[...] = x_ref[...] + y


def _merge(x, tok, qm, mk, mv, w_out, layer):
    b, t, d = x.shape
    tm = min(TOKEN_TILE, t)
    row = lambda width: pl.BlockSpec((None, tm, width), lambda bi, i: (bi, i, 0))
    if mk.ndim == 5:
        mem = pl.BlockSpec((None, None) + mk.shape[2:], lambda bi, i: (layer, bi, 0, 0, 0))
    else:
        mem = pl.BlockSpec((None,) + mk.shape[1:], lambda bi, i: (bi, 0, 0))
    return pl.pallas_call(
        _merge_kernel,
        grid=(b, t // tm),
        in_specs=[row(d), row(TOK_W), row(MEM_W), mem, mem,
                  pl.BlockSpec((None,) + w_out.shape[1:], lambda bi, i: (layer, 0, 0),
                               pipeline_mode=pl.Buffered(1))],
        out_specs=row(d),
        out_shape=jax.ShapeDtypeStruct((b, t, d), F32),
        scratch_shapes=[pltpu.VMEM((tm, MEM_W), BF16)],
        compiler_params=_params("parallel", "parallel"),
        name="merge",
    )(x, tok, qm, mk, mv, w_out)


def kernel(x_prompt, x_sample, mem_prompt, cache_a_k, cache_a_v, cache_b_k, cache_b_v, cache_mem_k, cache_mem_v, ffn1_norm, ffn1_w_in, ffn1_w_out, attn_norm, w_in, w_out, a_q_gain, a_k_gain, a_rel_bias, mem_norm, w_mem_kv, mem_q_gain, mem_k_gain, ffn2_norm, ffn2_w_in, ffn2_w_out):
    depth = w_in.shape[0]
    bp, tp, d = x_prompt.shape
    bs, ts, _ = x_sample.shape
    n_mem = mem_prompt.shape[1]
    past = cache_b_k.shape[2]
    a_keep = min(BAND_PAST, tp)

    ffn1_w_in, ffn1_w_out, ffn2_w_in, ffn2_w_out, w_in, w_out, w_mem_kv = (
        w.astype(BF16) for w in (ffn1_w_in, ffn1_w_out, ffn2_w_in, ffn2_w_out, w_in, w_out, w_mem_kv))

    x_p = x_prompt.reshape(bp * tp, d)
    x_s = x_sample.reshape(bs * ts, d)
    mem = mem_prompt.reshape(bp * n_mem, d)
    assert a_keep == min(TOKEN_TILE, tp), "band cache rows must be exactly the last token tile of a sequence"
    a_k_p, a_v_p, m_k_p, m_v_p = [], [], [], []
    n_a, n_b = (depth + 1) // 2, depth // 2

    def blank_kv(layers, rows, heads, dim):
        return [jnp.full((layers, rows, heads, dim), fill, F32) for fill in (0.0, 1.0)]

    new_kv_p = blank_kv(n_b, bp * tp, B_HEADS, B_DIM)
    new_kv_s = {True: blank_kv(n_a, bs * ts, A_HEADS, A_DIM), False: blank_kv(n_b, bs * ts, B_HEADS, B_DIM)}
    for l in range(depth):
        j = l // 2
        band = l % 2 == 0
        heads, dim = (A_HEADS, A_DIM) if band else (B_HEADS, B_DIM)
        x_p = _ffn(x_p, ffn1_norm[l], ffn1_w_in, ffn1_w_out, l)
        x_s = _ffn(x_s, ffn1_norm[l], ffn1_w_in, ffn1_w_out, l)

        segments = [(0, TOK_W, dim, 0 if band else None, dim ** -0.5 * (LOG2_E if band else 1.0)),
                    (TOK_W, TOK_W, dim, 1 if band else None, 1.0),
                    (2 * TOK_W, TOK_W, dim, None, 1.0),
                    (3 * TOK_W, MEM_W, MEM_DIM, 2 if band else 0, MEM_DIM ** -0.5)]
        gains = [a_q_gain[j], a_k_gain[j], mem_q_gain[l]] if band else [mem_q_gain[l]]
        dtypes = [[BF16], [BF16, F32], [BF16, F32], [BF16]]
        if band:
            q_p, k_p, kf_p, v_p, vf_p, qm_p = _proj(x_p, attn_norm[l], w_in, l, segments, gains, dtypes, tail_of=tp)
            a_k_p.append(kf_p.reshape(bp, a_keep, A_HEADS, A_DIM))
            a_v_p.append(vf_p.reshape(bp, a_keep, A_HEADS, A_DIM))
        else:
            q_p, k_p, kf_p, v_p, vf_p, qm_p = _proj(x_p, attn_norm[l], w_in, l, segments, gains, dtypes,
                                                    (j, new_kv_p))
            new_kv_p = [kf_p, vf_p]
        q_s, k_s, kf_s, v_s, vf_s, qm_s = _proj(x_s, attn_norm[l], w_in, l, segments, gains, dtypes,
                                                (j, new_kv_s[band]))
        new_kv_s[band] = [kf_s, vf_s]
        in_p = lambda a: a.reshape(bp, tp, -1)
        in_s = lambda a: a.reshape(bs, ts, -1)

        if band:
            tok_p = _band_prompt(in_p(q_p), in_p(k_p), in_p(v_p), a_rel_bias[j])
            tok_s = _band_step(in_s(q_s), cache_a_k, cache_a_v, j, in_s(k_s), in_s(v_s), a_rel_bias[j], past)
        else:
            tok_p = _sb_prompt(in_p(q_p), in_p(k_p), in_p(v_p))
            tok_s = _sb_step(in_s(q_s), cache_b_k, cache_b_v, j, in_s(k_s), in_s(v_s))

        mem_segments = [(0, MEM_W, MEM_DIM, 0, 1.0), (MEM_W, MEM_W, MEM_DIM, None, 1.0)]
        mk, mkf, mv, mvf = _proj(mem, mem_norm[l], w_mem_kv, l, mem_segments, [mem_k_gain[l]],
                                 [[BF16, F32], [BF16, F32]])
        m_k_p.append(mkf.reshape(bp, n_mem, N_MEM_HEADS, MEM_DIM))
        m_v_p.append(mvf.reshape(bp, n_mem, N_MEM_HEADS, MEM_DIM))

        x_p = _merge(in_p(x_p), tok_p, in_p(qm_p), mk.reshape(bp, n_mem, MEM_W), mv.reshape(bp, n_mem, MEM_W),
                     w_out, l).reshape(bp * tp, d)
        x_s = _merge(in_s(x_s), tok_s, in_s(qm_s), cache_mem_k, cache_mem_v, w_out, l).reshape(bs * ts, d)

        x_p = _ffn(x_p, ffn2_norm[l], ffn2_w_in, ffn2_w_out, l)
        x_s = _ffn(x_s, ffn2_norm[l], ffn2_w_in, ffn2_w_out, l)

    b_k_p, b_v_p = (c.reshape(-1, bp, tp, B_HEADS, B_DIM) for c in new_kv_p)
    a_k_s, a_v_s = (c.reshape(-1, bs, ts, A_HEADS, A_DIM) for c in new_kv_s[True])
    b_k_s, b_v_s = (c.reshape(-1, bs, ts, B_HEADS, B_DIM) for c in new_kv_s[False])
    return (x_p.reshape(bp, tp, d), x_s.reshape(bs, ts, d), jnp.stack(a_k_p), jnp.stack(a_v_p), b_k_p, b_v_p,
            jnp.stack(m_k_p), jnp.stack(m_v_p), a_k_s, a_v_s, b_k_s, b_v_s)
```

```python
import functools

import jax
import jax.numpy as jnp
import numpy as np
from jax import lax
from jax.experimental import pallas as pl
from jax.experimental.pallas import tpu as pltpu

CHUNK = 64
LEFT_CHUNKS = 8
BAND_PAST = LEFT_CHUNKS * CHUNK
REL_CLIP = 128
TOK_W = 1024
A_HEADS, A_DIM = 8, 128
B_HEADS, B_DIM = 4, 256
N_MEM_HEADS, MEM_DIM = 4, 128
MEM_W = N_MEM_HEADS * MEM_DIM
EPS = 1e-6
NEG_INF = -1e30
SB_NEG = -1e4
LOG2_E = 1.4426950408889634

V7X_VMEM_LIMIT_BYTES = 56 * 1024 * 1024
TOKEN_TILE = 512
FFN_ROW_TILE = 256
FF_TILE = 1024
BAND_Q_TILE = BAND_PAST // 2
SB_TILE = 256
SB_EXIT_MASS = 120.0

BF16 = jnp.bfloat16
F32 = jnp.float32


def _params(*semantics):
    return pltpu.CompilerParams(dimension_semantics=semantics, vmem_limit_bytes=V7X_VMEM_LIMIT_BYTES)


def _dot(a, b):
    return jnp.dot(a, b, preferred_element_type=F32)


def _dot_t(a, b):
    return lax.dot_general(a, b, (((1,), (1,)), ((), ())), preferred_element_type=F32)


def _normed_rows(x, g):
    r = lax.rsqrt(jnp.mean(x * x, axis=-1, keepdims=True) + EPS)
    return (x * r * g).astype(BF16)


def _softplus(z):
    return jnp.log(1.0 + jnp.exp(z))


def _ffn_kernel(x_ref, g_ref, wi_ref, wo_ref, o_ref, n_ref, x_keep_ref, *, chunk):
    i = pl.program_id(0)
    slot = i % 2
    f = wo_ref.shape[0]

    @pl.when(i == 0)
    def _():
        n_ref[0] = _normed_rows(x_ref[...], g_ref[...])
        x_keep_ref[...] = x_ref[...]

    @pl.when(i > 0)
    def _():
        n = n_ref[1 - slot]
        o_ref[...] = x_keep_ref[...]

        for c in range(0, f, chunk):
            a = _dot(n, wi_ref[:, c:c + chunk])
            b = _dot(n, wi_ref[:, f + c:f + c + chunk])
            h = (a / (1.0 + jnp.exp(-a)) * b * 0.5).astype(BF16)
            o_ref[...] += _dot(h, wo_ref[c:c + chunk, :])
        rows = x_ref[...]
        n_ref[slot] = _normed_rows(rows, g_ref[...])
        x_keep_ref[...] = rows


def _ffn(x, g, w_in, w_out, layer):
    n, d = x.shape
    f = w_out.shape[1]
    tm = min(FFN_ROW_TILE, n)
    n_tiles = n // tm
    resident = lambda shape: pl.BlockSpec((None,) + shape, lambda i: (layer, 0, 0), pipeline_mode=pl.Buffered(1))
    return pl.pallas_call(
        functools.partial(_ffn_kernel, chunk=min(FF_TILE, f)),
        grid=(n_tiles + 1,),
        in_specs=[
            pl.BlockSpec((tm, d), lambda i: (jnp.minimum(i, n_tiles - 1), 0)),
            pl.BlockSpec((1, d), lambda i: (0, 0)),
            resident(w_in.shape[1:]),
            resident(w_out.shape[1:]),
        ],
        out_specs=pl.BlockSpec((tm, d), lambda i: (jnp.maximum(i - 1, 0), 0)),
        out_shape=jax.ShapeDtypeStruct((n, d), F32),
        scratch_shapes=[pltpu.VMEM((2, tm, d), BF16), pltpu.VMEM((tm, d), F32)],
        compiler_params=_params("arbitrary"),
        name="ffn",
    )(x, g.reshape(1, d), w_in, w_out)


def _proj_kernel(*refs, segments, group, n_gain, n_carried):
    x_ref, g_ref, w_ref = refs[:3]
    gain_refs = refs[3:3 + n_gain]
    out_refs = refs[3 + n_gain + n_carried:]
    n = _normed_rows(x_ref[...], g_ref[...])
    groups, o = [], 0
    for col0, width, head_dim, gain_idx, scale, n_outs in segments:
        groups += [(col0 + c, c, head_dim, gain_idx, scale, out_refs[o:o + n_outs]) for c in range(0, width, group)]
        o += n_outs
    project = lambda k: _dot(n, w_ref[:, groups[k][0]:groups[k][0] + group])
    y_next = project(0)
    for k, (_, c, head_dim, gain_idx, scale, refs) in enumerate(groups):
        y = y_next
        if k + 1 < len(groups):
            y_next = project(k + 1)
        for h0 in range(0, group, head_dim):
            yh = y[:, h0:h0 + head_dim]
            if gain_idx is not None:
                r = lax.rsqrt(jnp.mean(yh * yh, axis=-1, keepdims=True) + EPS)
                yh = yh * r * gain_refs[gain_idx][...]
            if scale != 1.0:
                yh = yh * scale
            for ref in refs:
                if len(ref.shape) == 3:
                    ref[:, (c + h0) // head_dim, :] = yh.astype(ref.dtype)
                else:
                    ref[:, c + h0:c + h0 + head_dim] = yh.astype(ref.dtype)


def _proj(x, g, w, layer, segments, gains, out_dtypes, cache_slot=None, tail_of=None):
    n, d = x.shape
    tm = min(TOKEN_TILE, n)
    group = 512
    segs = tuple((c0, wd, hd, gi, sc, len(out_dtypes[s])) for s, (c0, wd, hd, gi, sc) in enumerate(segments))
    out_shape, out_specs, cache_outs = [], [], []
    for (c0, wd, hd, gi, sc), dts in zip(segments, out_dtypes):
        for dt in dts:
            if tail_of is not None and dt == F32:
                out_shape.append(jax.ShapeDtypeStruct((n // tail_of * tm, wd), dt))
                out_specs.append(pl.BlockSpec((tm, wd), functools.partial(lambda i, s: (i // s, 0), s=tail_of // tm)))
            elif cache_slot is not None and dt == F32:
                slot, carried = cache_slot
                assert carried[len(cache_outs)].shape[1:] == (n, wd // hd, hd)
                out_shape.append(jax.ShapeDtypeStruct(carried[len(cache_outs)].shape, dt))
                cache_outs.append(len(out_shape) - 1)
                out_specs.append(pl.BlockSpec((None, tm, wd // hd, hd),
                                              functools.partial(lambda i, s: (s, i, 0, 0), s=slot)))
            else:
                out_shape.append(jax.ShapeDtypeStruct((n, wd), dt))
                out_specs.append(pl.BlockSpec((tm, wd), lambda i: (i, 0)))
    carried = [] if cache_slot is None else list(cache_slot[1])
    n_fixed = 3 + len(gains)
    in_specs = [
        pl.BlockSpec((tm, d), lambda i: (i, 0)),
        pl.BlockSpec((1, d), lambda i: (0, 0)),
        pl.BlockSpec((None,) + w.shape[1:], lambda i: (layer, 0, 0), pipeline_mode=pl.Buffered(1)),
    ] + [pl.BlockSpec((1, gn.shape[-1]), lambda i: (0, 0)) for gn in gains] \
      + [pl.BlockSpec(memory_space=pl.ANY) for _ in carried]
    return pl.pallas_call(
        functools.partial(_proj_kernel, segments=segs, group=group, n_gain=len(gains), n_carried=len(carried)),
        grid=(n // tm,),
        in_specs=in_specs,
        out_specs=out_specs,
        out_shape=out_shape,
        input_output_aliases={n_fixed + c: cache_outs[c] for c in range(len(carried))},
        compiler_params=_params("arbitrary" if tail_of is not None else "parallel"),
        name="proj",
    )(x, g.reshape(1, d), w, *[gn.reshape(1, -1) for gn in gains], *carried)


def _band_kernel(q_ref, k0_ref, k1_ref, k2_ref, v0_ref, v1_ref, v2_ref, bias_ref, o_ref):
    cols = [slice(h * A_DIM, (h + 1) * A_DIM) for h in range(A_HEADS)]

    def scores(h):
        sl = cols[h]
        kh = jnp.concatenate([k0_ref[:, sl], k1_ref[:, sl], k2_ref[:, sl]], axis=0)
        return _dot_t(q_ref[:, sl], kh) + bias_ref[h]

    s_next = scores(0)
    for h in range(A_HEADS):
        s = s_next
        if h + 1 < A_HEADS:
            s_next = scores(h + 1)
        sl = cols[h]
        vh = jnp.concatenate([v0_ref[:, sl], v1_ref[:, sl], v2_ref[:, sl]], axis=0)
        p = jnp.exp2(s - jnp.max(s, axis=-1, keepdims=True))
        l = jnp.sum(p, axis=-1, keepdims=True)
        o_ref[:, sl] = (_dot(p.astype(BF16), vh) / l).astype(o_ref.dtype)


def _band_table(rel_bias, q0, nq, k0, nk):
    heads = rel_bias.shape[0]
    length = nq + nk - 1
    dist = (q0 - k0) + (nq - 1) - np.arange(length)
    n_far = int(np.sum(dist > REL_CLIP))
    n_ahead = int(np.sum(dist < -REL_CLIP))
    parts = [jnp.broadcast_to(rel_bias[:, -1:], (heads, n_far))]
    if n_far + n_ahead < length:
        hi = int(dist[n_far]) + REL_CLIP
        lo = int(dist[length - n_ahead - 1]) + REL_CLIP
        parts.append(rel_bias[:, lo:hi + 1][:, ::-1])
    parts.append(jnp.broadcast_to(rel_bias[:, :1], (heads, n_ahead + 1)))
    e = jnp.concatenate(parts, axis=1).astype(F32)
    period = length + 1
    rows = jnp.tile(e, (1, nq))[:, :nq * (period - 1)].reshape(heads, nq, period - 1)
    table = rows[:, :, nq - 1:nq - 1 + nk]
    q_pos = q0 + np.arange(nq)
    k_pos = k0 + np.arange(nk)
    qc = q_pos // CHUNK
    kc = k_pos // CHUNK
    valid = (k_pos[None, :] >= 0) & (kc[None, :] <= qc[:, None]) & (kc[None, :] >= qc[:, None] - LEFT_CHUNKS)
    return jnp.where(jnp.asarray(valid)[None], table * LOG2_E, NEG_INF)


def _band_prompt(q, k, v, rel_bias):
    b, t, w = q.shape
    tq = BAND_Q_TILE
    bias = jnp.stack([_band_table(rel_bias, i * tq, tq, (i - 2) * tq, 3 * tq) for i in range(3)])
    qspec = pl.BlockSpec((None, tq, w), lambda bi, i: (bi, i, 0))
    kspecs = [pl.BlockSpec((None, tq, w), functools.partial(lambda bi, i, d: (bi, jnp.maximum(i - d, 0), 0), d=d))
              for d in (2, 1, 0)]
    return pl.pallas_call(
        _band_kernel,
        grid=(b, t // tq),
        in_specs=[qspec] + kspecs + kspecs + [
            pl.BlockSpec((None,) + bias.shape[1:], lambda bi, i: (jnp.minimum(i, 2), 0, 0, 0))],
        out_specs=qspec,
        out_shape=jax.ShapeDtypeStruct((b, t, w), BF16),
        compiler_params=_params("parallel", "parallel"),
        name="band_prompt",
    )(q, k, k, k, v, v, v, bias)


def _band_step_kernel(q_ref, ck_ref, cv_ref, nk_ref, nv_ref, bc_ref, bn_ref, o_ref):
    ts = q_ref.shape[0]
    frames, heads, dim = ck_ref.shape
    head_cols = [slice(h * dim, (h + 1) * dim) for h in range(heads)]
    q = jnp.concatenate([q_ref[:, sl] for sl in head_cols], axis=0)
    sc = _dot_t(q, ck_ref[...].reshape(frames * heads, dim).astype(BF16)) + bc_ref[...]
    sn = jnp.concatenate([_dot_t(q_ref[:, sl], nk_ref[:, sl]) + bn_ref[h] for h, sl in enumerate(head_cols)], axis=0)
    m = jnp.maximum(jnp.max(sc, axis=-1, keepdims=True), jnp.max(sn, axis=-1, keepdims=True))
    pc = jnp.exp2(sc - m)
    pn = jnp.exp2(sn - m)
    l = jnp.sum(pc, axis=-1, keepdims=True) + jnp.sum(pn, axis=-1, keepdims=True)
    o = _dot(pc.astype(BF16), cv_ref[...].reshape(frames * heads, dim).astype(BF16))
    o = o + jnp.concatenate([_dot(pn[h * ts:(h + 1) * ts].astype(BF16), nv_ref[:, sl])
                             for h, sl in enumerate(head_cols)], axis=0)
    o = o / l
    for h, sl in enumerate(head_cols):
        o_ref[:, sl] = o[h * ts:(h + 1) * ts].astype(o_ref.dtype)


def _band_step(q, ck, cv, layer, nk, nv, rel_bias, past):
    n, ts, w = q.shape
    c = ck.shape[2]
    bias_c = _band_table(rel_bias, past, ts, past - c, c)
    bias_n = _band_table(rel_bias, past, ts, past, ts)
    same_head = jnp.eye(A_HEADS, dtype=bool)[:, None, None, :]
    bias_c = jnp.where(same_head, bias_c[:, :, :, None], NEG_INF).reshape(A_HEADS * ts, c * A_HEADS)
    new = pl.BlockSpec((None, ts, w), lambda i: (i, 0, 0))
    old = pl.BlockSpec((None, None, c, A_HEADS, A_DIM), lambda i: (layer, i, 0, 0, 0))
    return pl.pallas_call(
        _band_step_kernel,
        grid=(n,),
        in_specs=[new, old, old, new, new,
                  pl.BlockSpec(bias_c.shape, lambda i: (0, 0)),
                  pl.BlockSpec(bias_n.shape, lambda i: (0, 0, 0))],
        out_specs=new,
        out_shape=jax.ShapeDtypeStruct((n, ts, w), BF16),
        compiler_params=_params("parallel"),
        name="band_step",
    )(q, ck, cv, nk, nv, bias_c, bias_n)


def _suffix_ones(n):
    return (jnp.arange(n)[:, None] >= jnp.arange(n)[None, :]).astype(BF16)


def _sb_block(q, kb, vb, u, aft, mask):
    return _sb_weigh(_sb_logits(q, kb, mask), vb, u, aft)


def _sb_logits(q, kb, mask):
    z = _dot_t(q, kb)
    return z if mask is None else jnp.where(mask, z, SB_NEG)


def _sb_weigh(z, vb, u, aft):
    loc = _dot(_softplus(z).astype(BF16), u)
    w = jnp.exp(z - loc - aft)
    return _dot(w.astype(BF16), vb), aft + loc[:, 0:1]


def _sb_kernel(q_ref, k_ref, v_ref, u_ref, o_ref, acc_ref, aft_ref):
    t = q_ref.shape[0]
    i = pl.program_id(2)
    q = q_ref[...]
    u = u_ref[...]
    causal = lax.broadcasted_iota(jnp.int32, (t, t), 1) < lax.broadcasted_iota(jnp.int32, (t, t), 0)

    def block(j):
        rows = pl.ds(pl.multiple_of(j * t, t), t)
        return k_ref[rows, :], v_ref[rows, :]

    kd, vd = block(i)
    out_d, aft = _sb_block(q, kd, vd, u, 0.0, causal)
    kp, vp = block(jnp.maximum(i - 1, 0))
    out_p, aft = _sb_block(q, kp, vp, u, aft, i > 0)
    acc_ref[...] = out_d + out_p
    aft_ref[...] = aft

    def more(carry):
        j, least_aft = carry
        return jnp.logical_and(j >= 0, least_aft < SB_EXIT_MASS)

    def older(carry):
        j, _ = carry
        kb, vb = block(j)
        out, aft = _sb_block(q, kb, vb, u, aft_ref[...], None)
        acc_ref[...] += out
        aft_ref[...] = aft
        return j - 1, jnp.min(aft)

    lax.while_loop(more, older, (i - 2, jnp.min(aft)))
    o_ref[...] = acc_ref[...].astype(o_ref.dtype)


def _sb_walk(q, k, v):
    b, t, w = q.shape
    tq = min(SB_TILE, t)
    qspec = pl.BlockSpec((None, tq, B_DIM), lambda bi, h, i: (bi, i, h))
    kspec = pl.BlockSpec((None, t, B_DIM), lambda bi, h, i: (bi, 0, h))
    return pl.pallas_call(
        _sb_kernel,
        grid=(b, B_HEADS, t // tq),
        in_specs=[qspec, kspec, kspec, pl.BlockSpec((tq, tq), lambda bi, h, i: (0, 0))],
        out_specs=qspec,
        out_shape=jax.ShapeDtypeStruct((b, t, w), BF16),
        scratch_shapes=[pltpu.VMEM((tq, B_DIM), F32), pltpu.VMEM((tq, 1), F32)],
        compiler_params=_params("parallel", "parallel", "arbitrary"),
        name="sb_walk",
    )(q, k, v, _suffix_ones(tq))


def _sb_near_kernel(q_ref, kd_ref, vd_ref, kp_ref, vp_ref, u_ref, o_ref, least_ref):
    t = q_ref.shape[0]
    i = pl.program_id(1)
    u = u_ref[...]
    causal = lax.broadcasted_iota(jnp.int32, (t, t), 1) < lax.broadcasted_iota(jnp.int32, (t, t), 0)
    cols = [slice(h * B_DIM, (h + 1) * B_DIM) for h in range(B_HEADS)]

    def logits(h):
        q = q_ref[:, cols[h]]
        return _sb_logits(q, kd_ref[:, cols[h]], causal), _sb_logits(q, kp_ref[:, cols[h]], i > 0)

    least = None
    z_next = logits(0)
    for h, sl in enumerate(cols):
        z_d, z_p = z_next
        if h + 1 < B_HEADS:
            z_next = logits(h + 1)
        out_d, aft = _sb_weigh(z_d, vd_ref[:, sl], u, 0.0)
        out_p, aft = _sb_weigh(z_p, vp_ref[:, sl], u, aft)
        o_ref[:, sl] = (out_d + out_p).astype(o_ref.dtype)
        row_least = jnp.min(aft, axis=0, keepdims=True)
        least = row_least if least is None else jnp.minimum(least, row_least)
    least_ref[...] = jnp.broadcast_to(jnp.where(i >= 2, least, SB_EXIT_MASS), least_ref.shape)


def _sb_prompt(q, k, v):
    b, t, w = q.shape
    tq = min(SB_TILE, t)
    tile = pl.BlockSpec((None, tq, w), lambda bi, i: (bi, i, 0))
    before = pl.BlockSpec((None, tq, w), lambda bi, i: (bi, jnp.maximum(i - 1, 0), 0))
    near, least = pl.pallas_call(
        _sb_near_kernel,
        grid=(b, t // tq),
        in_specs=[tile, tile, tile, before, before, pl.BlockSpec((tq, tq), lambda bi, i: (0, 0))],
        out_specs=[tile, pl.BlockSpec((None, None, 8, 128), lambda bi, i: (bi, i, 0, 0))],
        out_shape=[jax.ShapeDtypeStruct((b, t, w), BF16), jax.ShapeDtypeStruct((b, t // tq, 8, 128), F32)],
        compiler_params=_params("parallel", "parallel"),
        name="sb_near",
    )(q, k, v, k, v, _suffix_ones(tq))
    return lax.cond(jnp.min(least) < SB_EXIT_MASS, lambda: _sb_walk(q, k, v), lambda: near)


def _sb_step_kernel(q_ref, ck_ref, cv_ref, nk_ref, nv_ref, un_ref, uc_ref, o_ref):
    ts = q_ref.shape[0]
    tk = uc_ref.shape[0]
    frames, heads, dim = ck_ref.shape
    causal = lax.broadcasted_iota(jnp.int32, (ts, ts), 1) < lax.broadcasted_iota(jnp.int32, (ts, ts), 0)
    head_cols = [slice(h * dim, (h + 1) * dim) for h in range(heads)]
    own = [_sb_block(q_ref[:, sl], nk_ref[:, sl], nv_ref[:, sl], un_ref[...], 0.0, causal) for sl in head_cols]
    q = jnp.concatenate([q_ref[:, sl] for sl in head_cols], axis=0)
    acc = jnp.concatenate([o for o, _ in own], axis=0)
    aft = jnp.concatenate([a for _, a in own], axis=0)
    own_head = (lax.broadcasted_iota(jnp.int32, (heads * ts, tk), 1) % heads
                == lax.broadcasted_iota(jnp.int32, (heads * ts, tk), 0) // ts)
    per_block = tk // heads
    block_rows = [slice(j * per_block, (j + 1) * per_block) for j in reversed(range(frames // per_block))]
    logits = lambda rows: _sb_logits(q, ck_ref[rows].reshape(tk, dim).astype(BF16), own_head)
    z_next = logits(block_rows[0])
    for k, rows in enumerate(block_rows):
        z = z_next
        if k + 1 < len(block_rows):
            z_next = logits(block_rows[k + 1])
        out, aft = _sb_weigh(z, cv_ref[rows].reshape(tk, dim).astype(BF16), uc_ref[...], aft)
        acc = acc + out
    for h, sl in enumerate(head_cols):
        o_ref[:, sl] = acc[h * ts:(h + 1) * ts].astype(o_ref.dtype)


def _sb_step(q, ck, cv, layer, nk, nv):
    n, ts, w = q.shape
    p = ck.shape[2]
    tk = min(SB_TILE, p)
    new = pl.BlockSpec((None, ts, w), lambda i: (i, 0, 0))
    old = pl.BlockSpec((None, None, p, B_HEADS, B_DIM), lambda i: (layer, i, 0, 0, 0))
    return pl.pallas_call(
        _sb_step_kernel,
        grid=(n,),
        in_specs=[new, old, old, new, new,
                  pl.BlockSpec((ts, ts), lambda i: (0, 0)),
                  pl.BlockSpec((tk, tk), lambda i: (0, 0))],
        out_specs=new,
        out_shape=jax.ShapeDtypeStruct((n, ts, w), BF16),
        compiler_params=_params("parallel"),
        name="sb_step",
    )(q, ck, cv, nk, nv, _suffix_ones(ts), _suffix_ones(tk))


def _merge_kernel(x_ref, tok_ref, qm_ref, mk_ref, mv_ref, w_ref, o_ref, mo_ref):
    head_cols = [slice(h * MEM_DIM, (h + 1) * MEM_DIM) for h in range(N_MEM_HEADS)]
    if len(mk_ref.shape) == 3:
        tm = qm_ref.shape[0]
        m_tok, heads, dim = mk_ref.shape
        q = jnp.concatenate([qm_ref[:, sl] for sl in head_cols], axis=0)
        s = _dot_t(q, mk_ref[...].reshape(m_tok * heads, dim).astype(BF16))
        own_head = (lax.broadcasted_iota(jnp.int32, s.shape, 1) % heads
                    == lax.broadcasted_iota(jnp.int32, s.shape, 0) // tm)
        s = jnp.where(own_head, s, NEG_INF)
        p = jnp.exp(s - jnp.max(s, axis=-1, keepdims=True))
        l = jnp.sum(p, axis=-1, keepdims=True)
        mo = _dot(p.astype(BF16), mv_ref[...].reshape(m_tok * heads, dim).astype(BF16)) / l
        for h, sl in enumerate(head_cols):
            mo_ref[:, sl] = mo[h * tm:(h + 1) * tm].astype(BF16)
    else:
        scores = lambda sl: _dot_t(qm_ref[:, sl], mk_ref[:, sl])
        s_next = scores(head_cols[0])
        for h, sl in enumerate(head_cols):
            s = s_next
            if h + 1 < N_MEM_HEADS:
                s_next = scores(head_cols[h + 1])
            p = jnp.exp(s - jnp.max(s, axis=-1, keepdims=True))
            l = jnp.sum(p, axis=-1, keepdims=True)
            mo_ref[:, sl] = (_dot(p.astype(BF16), mv_ref[:, sl]) / l).astype(BF16)
    y = _dot(tok_ref[...], w_ref[:TOK_W, :]) + _dot(mo_ref[...], w_ref[TOK_W:, :])
    o_ref[...] = x_ref[...] + y


def _merge(x, tok, qm, mk, mv, w_out, layer):
    b, t, d = x.shape
    tm = min(TOKEN_TILE, t)
    row = lambda width: pl.BlockSpec((None, tm, width), lambda bi, i: (bi, i, 0))
    if mk.ndim == 5:
        mem = pl.BlockSpec((None, None) + mk.shape[2:], lambda bi, i: (layer, bi, 0, 0, 0))
    else:
        mem = pl.BlockSpec((None,) + mk.shape[1:], lambda bi, i: (bi, 0, 0))
    return pl.pallas_call(
        _merge_kernel,
        grid=(b, t // tm),
        in_specs=[row(d), row(TOK_W), row(MEM_W), mem, mem,
                  pl.BlockSpec((None,) + w_out.shape[1:], lambda bi, i: (layer, 0, 0),
                               pipeline_mode=pl.Buffered(1))],
        out_specs=row(d),
        out_shape=jax.ShapeDtypeStruct((b, t, d), F32),
        scratch_shapes=[pltpu.VMEM((tm, MEM_W), BF16)],
        compiler_params=_params("parallel", "parallel"),
        name="merge",
    )(x, tok, qm, mk, mv, w_out)


def kernel(x_prompt, x_sample, mem_prompt, cache_a_k, cache_a_v, cache_b_k, cache_b_v, cache_mem_k, cache_mem_v, ffn1_norm, ffn1_w_in, ffn1_w_out, attn_norm, w_in, w_out, a_q_gain, a_k_gain, a_rel_bias, mem_norm, w_mem_kv, mem_q_gain, mem_k_gain, ffn2_norm, ffn2_w_in, ffn2_w_out):
    depth = w_in.shape[0]
    bp, tp, d = x_prompt.shape
    bs, ts, _ = x_sample.shape
    n_mem = mem_prompt.shape[1]
    past = cache_b_k.shape[2]
    a_keep = min(BAND_PAST, tp)

    ffn1_w_in, ffn1_w_out, ffn2_w_in, ffn2_w_out, w_in, w_out, w_mem_kv = (
        w.astype(BF16) for w in (ffn1_w_in, ffn1_w_out, ffn2_w_in, ffn2_w_out, w_in, w_out, w_mem_kv))

    x_p = x_prompt.reshape(bp * tp, d)
    x_s = x_sample.reshape(bs * ts, d)
    mem = mem_prompt.reshape(bp * n_mem, d)
    assert a_keep == min(TOKEN_TILE, tp), "band cache rows must be exactly the last token tile of a sequence"
    a_k_p, a_v_p, m_k_p, m_v_p = [], [], [], []
    n_a, n_b = (depth + 1) // 2, depth // 2

    def blank_kv(layers, rows, heads, dim):
        return [jnp.full((layers, rows, heads, dim), fill, F32) for fill in (0.0, 1.0)]

    new_kv_p = blank_kv(n_b, bp * tp, B_HEADS, B_DIM)
    new_kv_s = {True: blank_kv(n_a, bs * ts, A_HEADS, A_DIM), False: blank_kv(n_b, bs * ts, B_HEADS, B_DIM)}
    for l in range(depth):
        j = l // 2
        band = l % 2 == 0
        heads, dim = (A_HEADS, A_DIM) if band else (B_HEADS, B_DIM)
        x_p = _ffn(x_p, ffn1_norm[l], ffn1_w_in, ffn1_w_out, l)
        x_s = _ffn(x_s, ffn1_norm[l], ffn1_w_in, ffn1_w_out, l)

        segments = [(0, TOK_W, dim, 0 if band else None, dim ** -0.5 * (LOG2_E if band else 1.0)),
                    (TOK_W, TOK_W, dim, 1 if band else None, 1.0),
                    (2 * TOK_W, TOK_W, dim, None, 1.0),
                    (3 * TOK_W, MEM_W, MEM_DIM, 2 if band else 0, MEM_DIM ** -0.5)]
        gains = [a_q_gain[j], a_k_gain[j], mem_q_gain[l]] if band else [mem_q_gain[l]]
        dtypes = [[BF16], [BF16, F32], [BF16, F32], [BF16]]
        if band:
            q_p, k_p, kf_p, v_p, vf_p, qm_p = _proj(x_p, attn_norm[l], w_in, l, segments, gains, dtypes, tail_of=tp)
            a_k_p.append(kf_p.reshape(bp, a_keep, A_HEADS, A_DIM))
            a_v_p.append(vf_p.reshape(bp, a_keep, A_HEADS, A_DIM))
        else:
            q_p, k_p, kf_p, v_p, vf_p, qm_p = _proj(x_p, attn_norm[l], w_in, l, segments, gains, dtypes,
                                                    (j, new_kv_p))
            new_kv_p = [kf_p, vf_p]
        q_s, k_s, kf_s, v_s, vf_s, qm_s = _proj(x_s, attn_norm[l], w_in, l, segments, gains, dtypes,
                                                (j, new_kv_s[band]))
        new_kv_s[band] = [kf_s, vf_s]
        in_p = lambda a: a.reshape(bp, tp, -1)
        in_s = lambda a: a.reshape(bs, ts, -1)

        if band:
            tok_p = _band_prompt(in_p(q_p), in_p(k_p), in_p(v_p), a_rel_bias[j])
            tok_s = _band_step(in_s(q_s), cache_a_k, cache_a_v, j, in_s(k_s), in_s(v_s), a_rel_bias[j], past)
        else:
            tok_p = _sb_prompt(in_p(q_p), in_p(k_p), in_p(v_p))
            tok_s = _sb_step(in_s(q_s), cache_b_k, cache_b_v, j, in_s(k_s), in_s(v_s))

        mem_segments = [(0, MEM_W, MEM_DIM, 0, 1.0), (MEM_W, MEM_W, MEM_DIM, None, 1.0)]
        mk, mkf, mv, mvf = _proj(mem, mem_norm[l], w_mem_kv, l, mem_segments, [mem_k_gain[l]],
                                 [[BF16, F32], [BF16, F32]])
        m_k_p.append(mkf.reshape(bp, n_mem, N_MEM_HEADS, MEM_DIM))
        m_v_p.append(mvf.reshape(bp, n_mem, N_MEM_HEADS, MEM_DIM))

        x_p = _merge(in_p(x_p), tok_p, in_p(qm_p), mk.reshape(bp, n_mem, MEM_W), mv.reshape(bp, n_mem, MEM_W),
                     w_out, l).reshape(bp * tp, d)
        x_s = _merge(in_s(x_s), tok_s, in_s(qm_s), cache_mem_k, cache_mem_v, w_out, l).reshape(bs * ts, d)

        x_p = _ffn(x_p, ffn2_norm[l], ffn2_w_in, ffn2_w_out, l)
        x_s = _ffn(x_s, ffn2_norm[l], ffn2_w_in, ffn2_w_out, l)

    b_k_p, b_v_p = (c.reshape(-1, bp, tp, B_HEADS, B_DIM) for c in new_kv_p)
    a_k_s, a_v_s = (c.reshape(-1, bs, ts, A_HEADS, A_DIM) for c in new_kv_s[True])
    b_k_s, b_v_s = (c.reshape(-1, bs, ts, B_HEADS, B_DIM) for c in new_kv_s[False])
    return (x_p.reshape(bp, tp, d), x_s.reshape(bs, ts, d), jnp.stack(a_k_p), jnp.stack(a_v_p), b_k_p, b_v_p,
            jnp.stack(m_k_p), jnp.stack(m_v_p), a_k_s, a_v_s, b_k_s, b_v_s)
```

```python
import functools

import jax
import jax.numpy as jnp
import numpy as np
from jax import lax
from jax.experimental import pallas as pl
from jax.experimental.pallas import tpu as pltpu

CHUNK = 64
LEFT_CHUNKS = 8
BAND_PAST = LEFT_CHUNKS * CHUNK
REL_CLIP = 128
TOK_W = 1024
A_HEADS, A_DIM = 8, 128
B_HEADS, B_DIM = 4, 256
N_MEM_HEADS, MEM_DIM = 4, 128
MEM_W = N_MEM_HEADS * MEM_DIM
EPS = 1e-6
NEG_INF = -1e30
SB_NEG = -1e4
LOG2_E = 1.4426950408889634

V7X_VMEM_LIMIT_BYTES = 56 * 1024 * 1024
TOKEN_TILE = 512
FFN_ROW_TILE = 256
FF_TILE = 1024
FFN_CAST_TILE = 256
BAND_Q_TILE = BAND_PAST // 2
SB_TILE = 256
SB_EXIT_MASS = 120.0

BF16 = jnp.bfloat16
F32 = jnp.float32


def _params(*semantics):
    return pltpu.CompilerParams(dimension_semantics=semantics, vmem_limit_bytes=V7X_VMEM_LIMIT_BYTES)


def _dot(a, b):
    return jnp.dot(a, b, preferred_element_type=F32)


def _dot_t(a, b):
    return lax.dot_general(a, b, (((1,), (1,)), ((), ())), preferred_element_type=F32)


def _normed_rows(x, g):
    r = lax.rsqrt(jnp.mean(x * x, axis=-1, keepdims=True) + EPS)
    return (x * r * g).astype(BF16)


def _softplus(z):
    return jnp.log(1.0 + jnp.exp(z))


def _ffn_cast_kernel(x_ref, g_ref, wa_ref, wb_ref, wo_ref, o_ref, wa16_ref, wb16_ref, wo16_ref, n_ref):
    @pl.when(pl.program_id(0) == 0)
    def _():
        n_ref[...] = _normed_rows(x_ref[...], g_ref[...])
        o_ref[...] = x_ref[...]

    wa = wa_ref[...].astype(BF16)
    wb = wb_ref[...].astype(BF16)
    wo = wo_ref[...].astype(BF16)
    wa16_ref[...] = wa
    wb16_ref[...] = wb
    wo16_ref[...] = wo
    n = n_ref[...]
    a = _dot(n, wa)
    b = _dot(n, wb)
    h = (a / (1.0 + jnp.exp(-a)) * b * 0.5).astype(BF16)
    o_ref[...] += _dot(h, wo)


def _ffn_cast(x, g, w_in, w_out, layer):
    n, d = x.shape
    f = w_out.shape[1]
    assert n <= TOKEN_TILE, "the weight-casting SwiGLU handles a single row tile"
    tf = min(FFN_CAST_TILE, f)
    nf = f // tf
    rows = pl.BlockSpec((n, d), lambda j: (0, 0))
    return pl.pallas_call(
        _ffn_cast_kernel,
        grid=(nf,),
        in_specs=[rows, pl.BlockSpec((1, d), lambda j: (0, 0)),
                  pl.BlockSpec((None, d, tf), lambda j: (layer, 0, j)),
                  pl.BlockSpec((None, d, tf), lambda j: (layer, 0, j + nf)),
                  pl.BlockSpec((None, tf, d), lambda j: (layer, j, 0))],
        out_specs=[rows, pl.BlockSpec((d, tf), lambda j: (0, j)), pl.BlockSpec((d, tf), lambda j: (0, j)),
                   pl.BlockSpec((tf, d), lambda j: (j, 0))],
        out_shape=[jax.ShapeDtypeStruct((n, d), F32), jax.ShapeDtypeStruct((d, f), BF16),
                   jax.ShapeDtypeStruct((d, f), BF16), jax.ShapeDtypeStruct((f, d), BF16)],
        scratch_shapes=[pltpu.VMEM((n, d), BF16)],
        compiler_params=_params("arbitrary"),
        name="ffn_cast",
    )(x, g.reshape(1, d), w_in, w_in, w_out)


def _ffn_kernel(x_ref, g_ref, wa_ref, wb_ref, wo_ref, o_ref, n_ref, x_keep_ref, *, chunk):
    i = pl.program_id(0)
    slot = i % 2
    f = wo_ref.shape[0]

    @pl.when(i == 0)
    def _():
        n_ref[0] = _normed_rows(x_ref[...], g_ref[...])
        x_keep_ref[...] = x_ref[...]

    @pl.when(i > 0)
    def _():
        n = n_ref[1 - slot]
        o_ref[...] = x_keep_ref[...]

        for c in range(0, f, chunk):
            a = _dot(n, wa_ref[:, c:c + chunk])
            b = _dot(n, wb_ref[:, c:c + chunk])
            h = (a / (1.0 + jnp.exp(-a)) * b * 0.5).astype(BF16)
            o_ref[...] += _dot(h, wo_ref[c:c + chunk, :])
        rows = x_ref[...]
        n_ref[slot] = _normed_rows(rows, g_ref[...])
        x_keep_ref[...] = rows


def _ffn(x, g, w_gate, w_up, w_out):
    n, d = x.shape
    f = w_out.shape[0]
    tm = min(FFN_ROW_TILE, n)
    n_tiles = n // tm
    resident = lambda shape: pl.BlockSpec(shape, lambda i: (0, 0), pipeline_mode=pl.Buffered(1))
    return pl.pallas_call(
        functools.partial(_ffn_kernel, chunk=min(FF_TILE, f)),
        grid=(n_tiles + 1,),
        in_specs=[
            pl.BlockSpec((tm, d), lambda i: (jnp.minimum(i, n_tiles - 1), 0)),
            pl.BlockSpec((1, d), lambda i: (0, 0)),
            resident(w_gate.shape),
            resident(w_up.shape),
            resident(w_out.shape),
        ],
        out_specs=pl.BlockSpec((tm, d), lambda i: (jnp.maximum(i - 1, 0), 0)),
        out_shape=jax.ShapeDtypeStruct((n, d), F32),
        scratch_shapes=[pltpu.VMEM((2, tm, d), BF16), pltpu.VMEM((tm, d), F32)],
        compiler_params=_params("arbitrary"),
        name="ffn",
    )(x, g.reshape(1, d), w_gate, w_up, w_out)


def _proj_kernel(*refs, segments, group, n_gain, n_carried):
    x_ref, g_ref, w_ref = refs[:3]
    gain_refs = refs[3:3 + n_gain]
    out_refs = refs[3 + n_gain + n_carried:]
    n = _normed_rows(x_ref[...], g_ref[...])
    groups, o = [], 0
    for col0, width, head_dim, gain_idx, scale, n_outs in segments:
        groups += [(col0 + c, c, head_dim, gain_idx, scale, out_refs[o:o + n_outs]) for c in range(0, width, group)]
        o += n_outs
    project = lambda k: _dot(n, w_ref[:, groups[k][0]:groups[k][0] + group])
    y_next = project(0)
    for k, (_, c, head_dim, gain_idx, scale, refs) in enumerate(groups):
        y = y_next
        if k + 1 < len(groups):
            y_next = project(k + 1)
        for h0 in range(0, group, head_dim):
            yh = y[:, h0:h0 + head_dim]
            if gain_idx is not None:
                r = lax.rsqrt(jnp.mean(yh * yh, axis=-1, keepdims=True) + EPS)
                yh = yh * r * gain_refs[gain_idx][...]
            if scale != 1.0:
                yh = yh * scale
            for ref in refs:
                if len(ref.shape) == 3:
                    ref[:, (c + h0) // head_dim, :] = yh.astype(ref.dtype)
                else:
                    ref[:, c + h0:c + h0 + head_dim] = yh.astype(ref.dtype)


def _proj(x, g, w, layer, segments, gains, out_dtypes, cache_slot=None, tail_of=None):
    n, d = x.shape
    tm = min(TOKEN_TILE, n)
    group = 512
    segs = tuple((c0, wd, hd, gi, sc, len(out_dtypes[s])) for s, (c0, wd, hd, gi, sc) in enumerate(segments))
    out_shape, out_specs, cache_outs = [], [], []
    for (c0, wd, hd, gi, sc), dts in zip(segments, out_dtypes):
        for dt in dts:
            if tail_of is not None and dt == F32:
                out_shape.append(jax.ShapeDtypeStruct((n // tail_of * tm, wd), dt))
                out_specs.append(pl.BlockSpec((tm, wd), functools.partial(lambda i, s: (i // s, 0), s=tail_of // tm)))
            elif cache_slot is not None and dt == F32:
                slot, carried = cache_slot
                assert carried[len(cache_outs)].shape[1:] == (n, wd // hd, hd)
                out_shape.append(jax.ShapeDtypeStruct(carried[len(cache_outs)].shape, dt))
                cache_outs.append(len(out_shape) - 1)
                out_specs.append(pl.BlockSpec((None, tm, wd // hd, hd),
                                              functools.partial(lambda i, s: (s, i, 0, 0), s=slot)))
            else:
                out_shape.append(jax.ShapeDtypeStruct((n, wd), dt))
                out_specs.append(pl.BlockSpec((tm, wd), lambda i: (i, 0)))
    carried = [] if cache_slot is None else list(cache_slot[1])
    n_fixed = 3 + len(gains)
    in_specs = [
        pl.BlockSpec((tm, d), lambda i: (i, 0)),
        pl.BlockSpec((1, d), lambda i: (0, 0)),
        pl.BlockSpec((None,) + w.shape[1:], lambda i: (layer, 0, 0), pipeline_mode=pl.Buffered(1)),
    ] + [pl.BlockSpec((1, gn.shape[-1]), lambda i: (0, 0)) for gn in gains] \
      + [pl.BlockSpec(memory_space=pl.ANY) for _ in carried]
    return pl.pallas_call(
        functools.partial(_proj_kernel, segments=segs, group=group, n_gain=len(gains), n_carried=len(carried)),
        grid=(n // tm,),
        in_specs=in_specs,
        out_specs=out_specs,
        out_shape=out_shape,
        input_output_aliases={n_fixed + c: cache_outs[c] for c in range(len(carried))},
        compiler_params=_params("arbitrary" if tail_of is not None else "parallel"),
        name="proj",
    )(x, g.reshape(1, d), w, *[gn.reshape(1, -1) for gn in gains], *carried)


def _band_kernel(q_ref, k0_ref, k1_ref, k2_ref, v0_ref, v1_ref, v2_ref, bias_ref, o_ref):
    cols = [slice(h * A_DIM, (h + 1) * A_DIM) for h in range(A_HEADS)]

    def scores(h):
        sl = cols[h]
        kh = jnp.concatenate([k0_ref[:, sl], k1_ref[:, sl], k2_ref[:, sl]], axis=0)
        return _dot_t(q_ref[:, sl], kh) + bias_ref[h]

    s_next = scores(0)
    for h in range(A_HEADS):
        s = s_next
        if h + 1 < A_HEADS:
            s_next = scores(h + 1)
        sl = cols[h]
        vh = jnp.concatenate([v0_ref[:, sl], v1_ref[:, sl], v2_ref[:, sl]], axis=0)
        p = jnp.exp2(s - jnp.max(s, axis=-1, keepdims=True))
        l = jnp.sum(p, axis=-1, keepdims=True)
        o_ref[:, sl] = (_dot(p.astype(BF16), vh) / l).astype(o_ref.dtype)


def _band_table(rel_bias, q0, nq, k0, nk):
    heads = rel_bias.shape[0]
    length = nq + nk - 1
    dist = (q0 - k0) + (nq - 1) - np.arange(length)
    n_far = int(np.sum(dist > REL_CLIP))
    n_ahead = int(np.sum(dist < -REL_CLIP))
    parts = [jnp.broadcast_to(rel_bias[:, -1:], (heads, n_far))]
    if n_far + n_ahead < length:
        hi = int(dist[n_far]) + REL_CLIP
        lo = int(dist[length - n_ahead - 1]) + REL_CLIP
        parts.append(rel_bias[:, lo:hi + 1][:, ::-1])
    parts.append(jnp.broadcast_to(rel_bias[:, :1], (heads, n_ahead + 1)))
    e = jnp.concatenate(parts, axis=1).astype(F32)
    period = length + 1
    rows = jnp.tile(e, (1, nq))[:, :nq * (period - 1)].reshape(heads, nq, period - 1)
    table = rows[:, :, nq - 1:nq - 1 + nk]
    q_pos = q0 + np.arange(nq)
    k_pos = k0 + np.arange(nk)
    qc = q_pos // CHUNK
    kc = k_pos // CHUNK
    valid = (k_pos[None, :] >= 0) & (kc[None, :] <= qc[:, None]) & (kc[None, :] >= qc[:, None] - LEFT_CHUNKS)
    return jnp.where(jnp.asarray(valid)[None], table * LOG2_E, NEG_INF)


def _band_prompt(q, k, v, rel_bias):
    b, t, w = q.shape
    tq = BAND_Q_TILE
    bias = jnp.stack([_band_table(rel_bias, i * tq, tq, (i - 2) * tq, 3 * tq) for i in range(3)])
    qspec = pl.BlockSpec((None, tq, w), lambda bi, i: (bi, i, 0))
    kspecs = [pl.BlockSpec((None, tq, w), functools.partial(lambda bi, i, d: (bi, jnp.maximum(i - d, 0), 0), d=d))
              for d in (2, 1, 0)]
    return pl.pallas_call(
        _band_kernel,
        grid=(b, t // tq),
        in_specs=[qspec] + kspecs + kspecs + [
            pl.BlockSpec((None,) + bias.shape[1:], lambda bi, i: (jnp.minimum(i, 2), 0, 0, 0))],
        out_specs=qspec,
        out_shape=jax.ShapeDtypeStruct((b, t, w), BF16),
        compiler_params=_params("parallel", "parallel"),
        name="band_prompt",
    )(q, k, k, k, v, v, v, bias)


def _band_step_kernel(q_ref, ck_ref, cv_ref, nk_ref, nv_ref, bc_ref, bn_ref, o_ref):
    ts = q_ref.shape[0]
    frames, heads, dim = ck_ref.shape
    head_cols = [slice(h * dim, (h + 1) * dim) for h in range(heads)]
    q = jnp.concatenate([q_ref[:, sl] for sl in head_cols], axis=0)
    sc = _dot_t(q, ck_ref[...].reshape(frames * heads, dim).astype(BF16)) + bc_ref[...]
    sn = jnp.concatenate([_dot_t(q_ref[:, sl], nk_ref[:, sl]) + bn_ref[h] for h, sl in enumerate(head_cols)], axis=0)
    m = jnp.maximum(jnp.max(sc, axis=-1, keepdims=True), jnp.max(sn, axis=-1, keepdims=True))
    pc = jnp.exp2(sc - m)
    pn = jnp.exp2(sn - m)
    l = jnp.sum(pc, axis=-1, keepdims=True) + jnp.sum(pn, axis=-1, keepdims=True)
    o = _dot(pc.astype(BF16), cv_ref[...].reshape(frames * heads, dim).astype(BF16))
    o = o + jnp.concatenate([_dot(pn[h * ts:(h + 1) * ts].astype(BF16), nv_ref[:, sl])
                             for h, sl in enumerate(head_cols)], axis=0)
    o = o / l
    for h, sl in enumerate(head_cols):
        o_ref[:, sl] = o[h * ts:(h + 1) * ts].astype(o_ref.dtype)


def _band_step(q, ck, cv, layer, nk, nv, rel_bias, past):
    n, ts, w = q.shape
    c = ck.shape[2]
    bias_c = _band_table(rel_bias, past, ts, past - c, c)
    bias_n = _band_table(rel_bias, past, ts, past, ts)
    same_head = jnp.eye(A_HEADS, dtype=bool)[:, None, None, :]
    bias_c = jnp.where(same_head, bias_c[:, :, :, None], NEG_INF).reshape(A_HEADS * ts, c * A_HEADS)
    new = pl.BlockSpec((None, ts, w), lambda i: (i, 0, 0))
    old = pl.BlockSpec((None, None, c, A_HEADS, A_DIM), lambda i: (layer, i, 0, 0, 0))
    return pl.pallas_call(
        _band_step_kernel,
        grid=(n,),
        in_specs=[new, old, old, new, new,
                  pl.BlockSpec(bias_c.shape, lambda i: (0, 0)),
                  pl.BlockSpec(bias_n.shape, lambda i: (0, 0, 0))],
        out_specs=new,
        out_shape=jax.ShapeDtypeStruct((n, ts, w), BF16),
        compiler_params=_params("parallel"),
        name="band_step",
    )(q, ck, cv, nk, nv, bias_c, bias_n)


def _suffix_ones(n):
    return (jnp.arange(n)[:, None] >= jnp.arange(n)[None, :]).astype(BF16)


def _sb_block(q, kb, vb, u, aft, mask):
    return _sb_weigh(_sb_logits(q, kb, mask), vb, u, aft)


def _sb_logits(q, kb, mask):
    z = _dot_t(q, kb)
    return z if mask is None else jnp.where(mask, z, SB_NEG)


def _sb_weigh(z, vb, u, aft):
    loc = _dot(_softplus(z).astype(BF16), u)
    w = jnp.exp(z - loc - aft)
    return _dot(w.astype(BF16), vb), aft + loc[:, 0:1]


def _sb_kernel(q_ref, k_ref, v_ref, u_ref, o_ref, acc_ref, aft_ref):
    t = q_ref.shape[0]
    i = pl.program_id(2)
    q = q_ref[...]
    u = u_ref[...]
    causal = lax.broadcasted_iota(jnp.int32, (t, t), 1) < lax.broadcasted_iota(jnp.int32, (t, t), 0)

    def block(j):
        rows = pl.ds(pl.multiple_of(j * t, t), t)
        return k_ref[rows, :], v_ref[rows, :]

    kd, vd = block(i)
    out_d, aft = _sb_block(q, kd, vd, u, 0.0, causal)
    kp, vp = block(jnp.maximum(i - 1, 0))
    out_p, aft = _sb_block(q, kp, vp, u, aft, i > 0)
    acc_ref[...] = out_d + out_p
    aft_ref[...] = aft

    def more(carry):
        j, least_aft = carry
        return jnp.logical_and(j >= 0, least_aft < SB_EXIT_MASS)

    def older(carry):
        j, _ = carry
        kb, vb = block(j)
        out, aft = _sb_block(q, kb, vb, u, aft_ref[...], None)
        acc_ref[...] += out
        aft_ref[...] = aft
        return j - 1, jnp.min(aft)

    lax.while_loop(more, older, (i - 2, jnp.min(aft)))
    o_ref[...] = acc_ref[...].astype(o_ref.dtype)


def _sb_walk(q, k, v):
    b, t, w = q.shape
    tq = min(SB_TILE, t)
    qspec = pl.BlockSpec((None, tq, B_DIM), lambda bi, h, i: (bi, i, h))
    kspec = pl.BlockSpec((None, t, B_DIM), lambda bi, h, i: (bi, 0, h))
    return pl.pallas_call(
        _sb_kernel,
        grid=(b, B_HEADS, t // tq),
        in_specs=[qspec, kspec, kspec, pl.BlockSpec((tq, tq), lambda bi, h, i: (0, 0))],
        out_specs=qspec,
        out_shape=jax.ShapeDtypeStruct((b, t, w), BF16),
        scratch_shapes=[pltpu.VMEM((tq, B_DIM), F32), pltpu.VMEM((tq, 1), F32)],
        compiler_params=_params("parallel", "parallel", "arbitrary"),
        name="sb_walk",
    )(q, k, v, _suffix_ones(tq))


def _sb_near_kernel(q_ref, kd_ref, vd_ref, kp_ref, vp_ref, u_ref, o_ref, least_ref):
    t = q_ref.shape[0]
    i = pl.program_id(1)
    u = u_ref[...]
    causal = lax.broadcasted_iota(jnp.int32, (t, t), 1) < lax.broadcasted_iota(jnp.int32, (t, t), 0)
    cols = [slice(h * B_DIM, (h + 1) * B_DIM) for h in range(B_HEADS)]

    def logits(h):
        q = q_ref[:, cols[h]]
        return _sb_logits(q, kd_ref[:, cols[h]], causal), _sb_logits(q, kp_ref[:, cols[h]], i > 0)

    least = None
    z_next = logits(0)
    for h, sl in enumerate(cols):
        z_d, z_p = z_next
        if h + 1 < B_HEADS:
            z_next = logits(h + 1)
        out_d, aft = _sb_weigh(z_d, vd_ref[:, sl], u, 0.0)
        out_p, aft = _sb_weigh(z_p, vp_ref[:, sl], u, aft)
        o_ref[:, sl] = (out_d + out_p).astype(o_ref.dtype)
        row_least = jnp.min(aft, axis=0, keepdims=True)
        least = row_least if least is None else jnp.minimum(least, row_least)
    least_ref[...] = jnp.broadcast_to(jnp.where(i >= 2, least, SB_EXIT_MASS), least_ref.shape)


def _sb_prompt(q, k, v):
    b, t, w = q.shape
    tq = min(SB_TILE, t)
    tile = pl.BlockSpec((None, tq, w), lambda bi, i: (bi, i, 0))
    before = pl.BlockSpec((None, tq, w), lambda bi, i: (bi, jnp.maximum(i - 1, 0), 0))
    near, least = pl.pallas_call(
        _sb_near_kernel,
        grid=(b, t // tq),
        in_specs=[tile, tile, tile, before, before, pl.BlockSpec((tq, tq), lambda bi, i: (0, 0))],
        out_specs=[tile, pl.BlockSpec((None, None, 8, 128), lambda bi, i: (bi, i, 0, 0))],
        out_shape=[jax.ShapeDtypeStruct((b, t, w), BF16), jax.ShapeDtypeStruct((b, t // tq, 8, 128), F32)],
        compiler_params=_params("parallel", "parallel"),
        name="sb_near",
    )(q, k, v, k, v, _suffix_ones(tq))
    return lax.cond(jnp.min(least) < SB_EXIT_MASS, lambda: _sb_walk(q, k, v), lambda: near)


def _sb_step_kernel(q_ref, ck_ref, cv_ref, nk_ref, nv_ref, un_ref, uc_ref, o_ref):
    ts = q_ref.shape[0]
    tk = uc_ref.shape[0]
    frames, heads, dim = ck_ref.shape
    causal = lax.broadcasted_iota(jnp.int32, (ts, ts), 1) < lax.broadcasted_iota(jnp.int32, (ts, ts), 0)
    head_cols = [slice(h * dim, (h + 1) * dim) for h in range(heads)]
    own = [_sb_block(q_ref[:, sl], nk_ref[:, sl], nv_ref[:, sl], un_ref[...], 0.0, causal) for sl in head_cols]
    q = jnp.concatenate([q_ref[:, sl] for sl in head_cols], axis=0)
    acc = jnp.concatenate([o for o, _ in own], axis=0)
    aft = jnp.concatenate([a for _, a in own], axis=0)
    own_head = (lax.broadcasted_iota(jnp.int32, (heads * ts, tk), 1) % heads
                == lax.broadcasted_iota(jnp.int32, (heads * ts, tk), 0) // ts)
    per_block = tk // heads
    block_rows = [slice(j * per_block, (j + 1) * per_block) for j in reversed(range(frames // per_block))]
    logits = lambda rows: _sb_logits(q, ck_ref[rows].reshape(tk, dim).astype(BF16), own_head)
    z_next = logits(block_rows[0])
    for k, rows in enumerate(block_rows):
        z = z_next
        if k + 1 < len(block_rows):
            z_next = logits(block_rows[k + 1])
        out, aft = _sb_weigh(z, cv_ref[rows].reshape(tk, dim).astype(BF16), uc_ref[...], aft)
        acc = acc + out
    for h, sl in enumerate(head_cols):
        o_ref[:, sl] = acc[h * ts:(h + 1) * ts].astype(o_ref.dtype)


def _sb_step(q, ck, cv, layer, nk, nv):
    n, ts, w = q.shape
    p = ck.shape[2]
    tk = min(SB_TILE, p)
    new = pl.BlockSpec((None, ts, w), lambda i: (i, 0, 0))
    old = pl.BlockSpec((None, None, p, B_HEADS, B_DIM), lambda i: (layer, i, 0, 0, 0))
    return pl.pallas_call(
        _sb_step_kernel,
        grid=(n,),
        in_specs=[new, old, old, new, new,
                  pl.BlockSpec((ts, ts), lambda i: (0, 0)),
                  pl.BlockSpec((tk, tk), lambda i: (0, 0))],
        out_specs=new,
        out_shape=jax.ShapeDtypeStruct((n, ts, w), BF16),
        compiler_params=_params("parallel"),
        name="sb_step",
    )(q, ck, cv, nk, nv, _suffix_ones(ts), _suffix_ones(tk))


def _merge_kernel(x_ref, tok_ref, qm_ref, mk_ref, mv_ref, w_ref, o_ref, mo_ref):
    head_cols = [slice(h * MEM_DIM, (h + 1) * MEM_DIM) for h in range(N_MEM_HEADS)]
    if len(mk_ref.shape) == 3:
        tm = qm_ref.shape[0]
        m_tok, heads, dim = mk_ref.shape
        q = jnp.concatenate([qm_ref[:, sl] for sl in head_cols], axis=0)
        s = _dot_t(q, mk_ref[...].reshape(m_tok * heads, dim).astype(BF16))
        own_head = (lax.broadcasted_iota(jnp.int32, s.shape, 1) % heads
                    == lax.broadcasted_iota(jnp.int32, s.shape, 0) // tm)
        s = jnp.where(own_head, s, NEG_INF)
        p = jnp.exp(s - jnp.max(s, axis=-1, keepdims=True))
        l = jnp.sum(p, axis=-1, keepdims=True)
        mo = _dot(p.astype(BF16), mv_ref[...].reshape(m_tok * heads, dim).astype(BF16)) / l
        for h, sl in enumerate(head_cols):
            mo_ref[:, sl] = mo[h * tm:(h + 1) * tm].astype(BF16)
    else:
        scores = lambda sl: _dot_t(qm_ref[:, sl], mk_ref[:, sl])
        s_next = scores(head_cols[0])
        for h, sl in enumerate(head_cols):
            s = s_next
            if h + 1 < N_MEM_HEADS:
                s_next = scores(head_cols[h + 1])
            p = jnp.exp(s - jnp.max(s, axis=-1, keepdims=True))
            l = jnp.sum(p, axis=-1, keepdims=True)
            mo_ref[:, sl] = (_dot(p.astype(BF16), mv_ref[:, sl]) / l).astype(BF16)
    y = _dot(tok_ref[...], w_ref[:TOK_W, :]) + _dot(mo_ref[...], w_ref[TOK_W:, :])
    o_ref[...] = x_ref[...] + y


def _merge(x, tok, qm, mk, mv, w_out, layer):
    b, t, d = x.shape
    tm = min(TOKEN_TILE, t)
    row = lambda width: pl.BlockSpec((None, tm, width), lambda bi, i: (bi, i, 0))
    if mk.ndim == 5:
        mem = pl.BlockSpec((None, None) + mk.shape[2:], lambda bi, i: (layer, bi, 0, 0, 0))
    else:
        mem = pl.BlockSpec((None,) + mk.shape[1:], lambda bi, i: (bi, 0, 0))
    return pl.pallas_call(
        _merge_kernel,
        grid=(b, t // tm),
        in_specs=[row(d), row(TOK_W), row(MEM_W), mem, mem,
                  pl.BlockSpec((None,) + w_out.shape[1:], lambda bi, i: (layer, 0, 0),
                               pipeline_mode=pl.Buffered(1))],
        out_specs=row(d),
        out_shape=jax.ShapeDtypeStruct((b, t, d), F32),
        scratch_shapes=[pltpu.VMEM((tm, MEM_W), BF16)],
        compiler_params=_params("parallel", "parallel"),
        name="merge",
    )(x, tok, qm, mk, mv, w_out)


def kernel(x_prompt, x_sample, mem_prompt, cache_a_k, cache_a_v, cache_b_k, cache_b_v, cache_mem_k, cache_mem_v, ffn1_norm, ffn1_w_in, ffn1_w_out, attn_norm, w_in, w_out, a_q_gain, a_k_gain, a_rel_bias, mem_norm, w_mem_kv, mem_q_gain, mem_k_gain, ffn2_norm, ffn2_w_in, ffn2_w_out):
    depth = w_in.shape[0]
    bp, tp, d = x_prompt.shape
    bs, ts, _ = x_sample.shape
    n_mem = mem_prompt.shape[1]
    past = cache_b_k.shape[2]
    a_keep = min(BAND_PAST, tp)

    w_in, w_out, w_mem_kv = (w.astype(BF16) for w in (w_in, w_out, w_mem_kv))

    x_p = x_prompt.reshape(bp * tp, d)
    x_s = x_sample.reshape(bs * ts, d)
    mem = mem_prompt.reshape(bp * n_mem, d)
    assert a_keep == min(TOKEN_TILE, tp), "band cache rows must be exactly the last token tile of a sequence"
    a_k_p, a_v_p, m_k_p, m_v_p = [], [], [], []
    n_a, n_b = (depth + 1) // 2, depth // 2

    def blank_kv(layers, rows, heads, dim):
        return [jnp.full((layers, rows, heads, dim), fill, F32) for fill in (0.0, 1.0)]

    new_kv_p = blank_kv(n_b, bp * tp, B_HEADS, B_DIM)
    new_kv_s = {True: blank_kv(n_a, bs * ts, A_HEADS, A_DIM), False: blank_kv(n_b, bs * ts, B_HEADS, B_DIM)}
    for l in range(depth):
        j = l // 2
        band = l % 2 == 0
        heads, dim = (A_HEADS, A_DIM) if band else (B_HEADS, B_DIM)
        x_s, *ffn1_bf16 = _ffn_cast(x_s, ffn1_norm[l], ffn1_w_in, ffn1_w_out, l)
        x_p = _ffn(x_p, ffn1_norm[l], *ffn1_bf16)

        segments = [(0, TOK_W, dim, 0 if band else None, dim ** -0.5 * (LOG2_E if band else 1.0)),
                    (TOK_W, TOK_W, dim, 1 if band else None, 1.0),
                    (2 * TOK_W, TOK_W, dim, None, 1.0),
                    (3 * TOK_W, MEM_W, MEM_DIM, 2 if band else 0, MEM_DIM ** -0.5)]
        gains = [a_q_gain[j], a_k_gain[j], mem_q_gain[l]] if band else [mem_q_gain[l]]
        dtypes = [[BF16], [BF16, F32], [BF16, F32], [BF16]]
        if band:
            q_p, k_p, kf_p, v_p, vf_p, qm_p = _proj(x_p, attn_norm[l], w_in, l, segments, gains, dtypes, tail_of=tp)
            a_k_p.append(kf_p.reshape(bp, a_keep, A_HEADS, A_DIM))
            a_v_p.append(vf_p.reshape(bp, a_keep, A_HEADS, A_DIM))
        else:
            q_p, k_p, kf_p, v_p, vf_p, qm_p = _proj(x_p, attn_norm[l], w_in, l, segments, gains, dtypes,
                                                    (j, new_kv_p))
            new_kv_p = [kf_p, vf_p]
        q_s, k_s, kf_s, v_s, vf_s, qm_s = _proj(x_s, attn_norm[l], w_in, l, segments, gains, dtypes,
                                                (j, new_kv_s[band]))
        new_kv_s[band] = [kf_s, vf_s]
        in_p = lambda a: a.reshape(bp, tp, -1)
        in_s = lambda a: a.reshape(bs, ts, -1)

        if band:
            tok_p = _band_prompt(in_p(q_p), in_p(k_p), in_p(v_p), a_rel_bias[j])
            tok_s = _band_step(in_s(q_s), cache_a_k, cache_a_v, j, in_s(k_s), in_s(v_s), a_rel_bias[j], past)
        else:
            tok_p = _sb_prompt(in_p(q_p), in_p(k_p), in_p(v_p))
            tok_s = _sb_step(in_s(q_s), cache_b_k, cache_b_v, j, in_s(k_s), in_s(v_s))

        mem_segments = [(0, MEM_W, MEM_DIM, 0, 1.0), (MEM_W, MEM_W, MEM_DIM, None, 1.0)]
        mk, mkf, mv, mvf = _proj(mem, mem_norm[l], w_mem_kv, l, mem_segments, [mem_k_gain[l]],
                                 [[BF16, F32], [BF16, F32]])
        m_k_p.append(mkf.reshape(bp, n_mem, N_MEM_HEADS, MEM_DIM))
        m_v_p.append(mvf.reshape(bp, n_mem, N_MEM_HEADS, MEM_DIM))

        x_p = _merge(in_p(x_p), tok_p, in_p(qm_p), mk.reshape(bp, n_mem, MEM_W), mv.reshape(bp, n_mem, MEM_W),
                     w_out, l).reshape(bp * tp, d)
        x_s = _merge(in_s(x_s), tok_s, in_s(qm_s), cache_mem_k, cache_mem_v, w_out, l).reshape(bs * ts, d)

        x_s, *ffn2_bf16 = _ffn_cast(x_s, ffn2_norm[l], ffn2_w_in, ffn2_w_out, l)
        x_p = _ffn(x_p, ffn2_norm[l], *ffn2_bf16)

    b_k_p, b_v_p = (c.reshape(-1, bp, tp, B_HEADS, B_DIM) for c in new_kv_p)
    a_k_s, a_v_s = (c.reshape(-1, bs, ts, A_HEADS, A_DIM) for c in new_kv_s[True])
    b_k_s, b_v_s = (c.reshape(-1, bs, ts, B_HEADS, B_DIM) for c in new_kv_s[False])
    return (x_p.reshape(bp, tp, d), x_s.reshape(bs, ts, d), jnp.stack(a_k_p), jnp.stack(a_v_p), b_k_p, b_v_p,
            jnp.stack(m_k_p), jnp.stack(m_v_p), a_k_s, a_v_s, b_k_s, b_v_s)
```
